```python
import math
import jax, jax.numpy as jnp
from jax import lax
import numpy as np

D_MODEL = 1024
BATCH = 8
SEQ = 2048
DEPTH = 4
DEC_BATCH = 128
DEC_SEQ = 4
PAST_LEN = 16384
PAGE_SIZE = 128

CHUNK = 128
A_GROUPS = 8
A_GROUP_DIM = 128
D_A = A_GROUPS * A_GROUP_DIM
D_B = D_MODEL
CONV_W = 31
N_MEM = 256
M_HEADS = 4
M_HEAD_DIM = 256
D_M = M_HEADS * M_HEAD_DIM
PEER_HEADS = 8
PEER_KEY_DIM = 128
PEER_HALF = PEER_KEY_DIM // 2
N_KEYS = 128
N_EXPERTS = N_KEYS * N_KEYS
PEER_TOPK = 16
PEER_BLOCK = 256
EPS = 1e-6
SPLITS = [D_A, 2 * D_A, 2 * D_A + D_B, 2 * D_A + 2 * D_B, 2 * D_A + 2 * D_B + D_M,
          2 * D_A + 2 * D_B + D_M + D_MODEL, 2 * D_A + 2 * D_B + D_M + 2 * D_MODEL]
D_IN = 2 * D_A + 2 * D_B + D_M + 3 * D_MODEL

kernel_name = 'hybrid_sgu_conformer_peer_decoder_step'


def _rmsnorm(x, g):
    xf = x.astype(jnp.float32)
    y = xf * lax.rsqrt(jnp.mean(xf * xf, axis=-1, keepdims=True) + EPS)
    return (y * g.astype(jnp.float32)).astype(x.dtype)


def _layernorm(x, g, b):
    xf = x.astype(jnp.float32)
    mu = jnp.mean(xf, axis=-1, keepdims=True)
    var = jnp.mean(jnp.square(xf - mu), axis=-1, keepdims=True)
    y = (xf - mu) * lax.rsqrt(var + EPS)
    return (y * g.astype(jnp.float32) + b.astype(jnp.float32)).astype(x.dtype)


def _spatial_gate(u, v, ws, bs, ln_g, ln_b):
    n, t, _ = u.shape
    L = min(t, CHUNK)
    nc = t // L
    vn = _layernorm(v, ln_g, ln_b)
    vc = vn.reshape(n, nc, L, A_GROUPS, A_GROUP_DIM)
    w = jnp.tril(ws[:, :L, :L])
    s = jnp.einsum('gts,ncsgd->nctgd', w, vc) + bs[:, :L].T[None, None, :, :, None]
    return u * s.reshape(n, t, D_A), vn


def _causal_depthwise(x_ext, w, b):
    y = lax.conv_general_dilated(x_ext, w[:, None, :], (1,), 'VALID',
                                 dimension_numbers=('NWC', 'WIO', 'NWC'),
                                 feature_group_count=x_ext.shape[-1])
    return y + b


def _mem_attend(q, mk, mv):
    n, t, _ = q.shape
    qh = q.reshape(n, t, M_HEADS, M_HEAD_DIM)
    s = jnp.einsum('nthd,nmhd->nhtm', qh, mk).astype(jnp.float32) * (M_HEAD_DIM ** -0.5)
    p = jax.nn.softmax(s, axis=-1).astype(q.dtype)
    o = jnp.einsum('nhtm,nmhd->nthd', p, mv)
    return o.reshape(n, t, D_M)


def _mixing_sublayer(x, mem_k, mem_v, conv_past, w):
    xn = _rmsnorm(x, w['norm_mix_g'])
    h = xn @ w['w_in']
    u_a, v_a, val_b, gate_b, q_m, gl_a, gl_b, gl_m = jnp.split(h, SPLITS, axis=-1)
    y_a, vn = _spatial_gate(jax.nn.gelu(u_a, approximate=False), jax.nn.gelu(v_a, approximate=False),
                            w['a_ws'], w['a_bs'], w['a_ln_g'], w['a_ln_b'])
    glu = val_b * jax.nn.sigmoid(gate_b)
    ext = jnp.concatenate([conv_past, glu], axis=1)
    new_conv = ext[:, -(CONV_W - 1):]
    c = _causal_depthwise(ext, w['b_conv_w'], w['b_conv_b'])
    y_b = jax.nn.silu(_layernorm(c, w['b_ln_g'], w['b_ln_b']))
    y_m = _mem_attend(q_m, mem_k, mem_v)
    merged = (jax.nn.sigmoid(gl_a) * (y_a @ w['w_o_a'])
              + jax.nn.sigmoid(gl_b) * (y_b @ w['w_o_b'])
              + jax.nn.sigmoid(gl_m) * (y_m @ w['w_o_m']))
    return x + merged @ w['w_out'], vn, new_conv


def _peer(xn, wq, keys, u_tab, v_tab):
    n, t, d = xn.shape
    n_tok = n * t
    n_blk = -(-n_tok // PEER_BLOCK)
    pad = n_blk * PEER_BLOCK - n_tok
    xt = jnp.pad(xn.reshape(n_tok, d), ((0, pad), (0, 0))).reshape(n_blk, PEER_BLOCK, d)

    def block(xb):
        q = (xb @ wq).reshape(PEER_BLOCK, PEER_HEADS, 2, PEER_HALF)
        s = jnp.einsum('bhpk,hpnk->bhpn', q, keys)
        v1, i1 = lax.top_k(s[:, :, 0], PEER_TOPK)
        v2, i2 = lax.top_k(s[:, :, 1], PEER_TOPK)
        cand = (v1[..., :, None] + v2[..., None, :]).reshape(PEER_BLOCK, PEER_HEADS, PEER_TOPK * PEER_TOPK)
        cv, ci = lax.top_k(cand, PEER_TOPK)
        e1 = jnp.take_along_axis(i1, ci // PEER_TOPK, axis=-1)
        e2 = jnp.take_along_axis(i2, ci % PEER_TOPK, axis=-1)
        eid = e1 * N_KEYS + e2
        g = jax.nn.softmax(cv.astype(jnp.float32), axis=-1).astype(xb.dtype)
        hid = jax.nn.gelu(jnp.einsum('bd,bhkd->bhk', xb, u_tab[eid]), approximate=False)
        return jnp.einsum('bhk,bhkd->bd', g * hid, v_tab[eid])

    y = lax.map(block, xt)
    return y.reshape(n_blk * PEER_BLOCK, d)[:n_tok].reshape(n, t, d)


def setup_inputs(seed: int = 0) -> dict:
    key = jax.random.key(seed)
    ks = iter(jax.random.split(key, 40))
    f32 = jnp.float32

    def nrm(shape, scale):
        return jax.random.normal(next(ks), shape, f32) * scale

    def gain(shape):
        return 1.0 + 0.05 * jax.random.normal(next(ks), shape, f32)

    return {
        'x_prompt': nrm((BATCH, SEQ, D_MODEL), 1.0),
        'x_sample': nrm((DEC_BATCH, DEC_SEQ, D_MODEL), 1.0),
        'mem_prompt': nrm((BATCH, N_MEM, D_MODEL), 1.0),
        'cache_mem_k': nrm((DEPTH, DEC_BATCH, N_MEM, M_HEADS, M_HEAD_DIM), 1.0),
        'cache_mem_v': nrm((DEPTH, DEC_BATCH, N_MEM, M_HEADS, M_HEAD_DIM), 1.0),
        'state_conv': nrm((DEPTH, DEC_BATCH, CONV_W - 1, D_B), 0.5),
        'norm_mix_g': gain((DEPTH, D_MODEL)),
        'w_in': nrm((DEPTH, D_MODEL, D_IN), D_MODEL ** -0.5),
        'a_ln_g': gain((DEPTH, D_A)),
        'a_ln_b': nrm((DEPTH, D_A), 0.02),
        'a_ws': nrm((DEPTH, A_GROUPS, CHUNK, CHUNK), 0.5 * CHUNK ** -0.5),
        'a_bs': gain((DEPTH, A_GROUPS, CHUNK)),
        'b_conv_w': nrm((DEPTH, CONV_W, D_B), CONV_W ** -0.5),
        'b_conv_b': nrm((DEPTH, D_B), 0.02),
        'b_ln_g': gain((DEPTH, D_B)),
        'b_ln_b': nrm((DEPTH, D_B), 0.02),
        'mem_norm_g': gain((DEPTH, D_MODEL)),
        'w_mem_k': nrm((DEPTH, D_MODEL, D_M), D_MODEL ** -0.5),
        'w_mem_v': nrm((DEPTH, D_MODEL, D_M), D_MODEL ** -0.5),
        'w_o_a': nrm((DEPTH, D_A, D_MODEL), D_A ** -0.5),
        'w_o_b': nrm((DEPTH, D_B, D_MODEL), D_B ** -0.5),
        'w_o_m': nrm((DEPTH, D_M, D_MODEL), D_M ** -0.5),
        'w_out': nrm((DEPTH, D_MODEL, D_MODEL), 0.5 * D_MODEL ** -0.5),
        'norm_ffn_g': gain((DEPTH, D_MODEL)),
        'peer_wq': nrm((DEPTH, D_MODEL, PEER_HEADS * PEER_KEY_DIM), D_MODEL ** -0.5),
        'peer_keys': nrm((DEPTH, PEER_HEADS, 2, N_KEYS, PEER_HALF), PEER_HALF ** -0.5),
        'peer_u': nrm((DEPTH, N_EXPERTS, D_MODEL), D_MODEL ** -0.5),
        'peer_v': nrm((DEPTH, N_EXPERTS, D_MODEL), (PEER_HEADS * PEER_TOPK) ** -0.5),
        'final_norm_g': gain((D_MODEL,)),
    }


def reference(x_prompt, x_sample, mem_prompt, cache_mem_k, cache_mem_v, state_conv,
              norm_mix_g, w_in, a_ln_g, a_ln_b, a_ws, a_bs, b_conv_w, b_conv_b, b_ln_g, b_ln_b,
              mem_norm_g, w_mem_k, w_mem_v, w_o_a, w_o_b, w_o_m, w_out,
              norm_ffn_g, peer_wq, peer_keys, peer_u, peer_v, final_norm_g):
    xp, xs = x_prompt, x_sample
    n_p, n_mem = mem_prompt.shape[0], mem_prompt.shape[1]
    mk_p, mv_p, conv_p, conv_s, chunk_v_s = [], [], [], [], []
    for l in range(DEPTH):
        lw = {'norm_mix_g': norm_mix_g[l], 'w_in': w_in[l], 'a_ln_g': a_ln_g[l], 'a_ln_b': a_ln_b[l],
              'a_ws': a_ws[l], 'a_bs': a_bs[l], 'b_conv_w': b_conv_w[l], 'b_conv_b': b_conv_b[l],
              'b_ln_g': b_ln_g[l], 'b_ln_b': b_ln_b[l], 'w_o_a': w_o_a[l], 'w_o_b': w_o_b[l],
              'w_o_m': w_o_m[l], 'w_out': w_out[l]}
        mem_n = _rmsnorm(mem_prompt, mem_norm_g[l])
        kp = (mem_n @ w_mem_k[l]).reshape(n_p, n_mem, M_HEADS, M_HEAD_DIM)
        vp = (mem_n @ w_mem_v[l]).reshape(n_p, n_mem, M_HEADS, M_HEAD_DIM)
        zero_past = jnp.zeros((xp.shape[0], CONV_W - 1, D_B), xp.dtype)
        xp, _, cp = _mixing_sublayer(xp, kp, vp, zero_past, lw)
        xp = xp + _peer(_rmsnorm(xp, norm_ffn_g[l]), peer_wq[l], peer_keys[l], peer_u[l], peer_v[l])
        xs, vns, cs = _mixing_sublayer(xs, cache_mem_k[l], cache_mem_v[l], state_conv[l], lw)
        xs = xs + _peer(_rmsnorm(xs, norm_ffn_g[l]), peer_wq[l], peer_keys[l], peer_u[l], peer_v[l])
        mk_p.append(kp)
        mv_p.append(vp)
        conv_p.append(cp)
        conv_s.append(cs)
        chunk_v_s.append(vns)
    y_prompt = _rmsnorm(xp, final_norm_g)
    y_sample = _rmsnorm(xs, final_norm_g)
    new_mem_k_prompt = jnp.stack(mk_p)
    new_mem_v_prompt = jnp.stack(mv_p)
    new_conv_prompt = jnp.stack(conv_p)
    new_conv_sample = jnp.stack(conv_s)
    new_chunk_v_sample = jnp.stack(chunk_v_s)
    return (y_prompt, y_sample, new_mem_k_prompt, new_mem_v_prompt, new_conv_prompt, new_conv_sample, new_chunk_v_sample)
```

```python
import functools
import math

import jax
import jax.numpy as jnp
from jax import lax
from jax.experimental import pallas as pl
from jax.experimental.pallas import tpu as pltpu

F32 = jnp.float32
BF16 = jnp.bfloat16
EPS = 1e-6
LANES = 128
VMEM_LIMIT_BYTES = 56 * 1024 * 1024

CHUNK = 128
A_GROUPS = 8
CONV_W = 31
CONV_HALO = 32
M_HEADS = 4
PEER_HEADS = 8
N_KEYS = 128
PEER_HALF = 64
PEER_TOPK = 16
NEG_INF = float("-inf")

_CAND = [(a, b) for a in range(PEER_TOPK) for b in range(PEER_TOPK) if (a + 1) * (b + 1) <= PEER_TOPK]
_CAND_ROWS = -(-len(_CAND) // 8) * 8


def _tile(n, target):
    best = None
    t = LANES
    while t <= min(n, target):
        if n % t == 0:
            best = t
        t += LANES
    assert best is not None, (n, target)
    return best


def _params(*sem):
    return pltpu.CompilerParams(dimension_semantics=sem, vmem_limit_bytes=VMEM_LIMIT_BYTES)


def _gelu(x):
    return 0.5 * x * (1.0 + lax.erf(x * (2.0 ** -0.5)))


def _sigmoid(x):
    return 1.0 / (1.0 + jnp.exp(-x))


def _rms(x, g):
    ms = jnp.mean(x * x, axis=-1, keepdims=True)
    return x * lax.rsqrt(ms + EPS) * g


def _layernorm(x, g, b):
    mu = jnp.mean(x, axis=-1, keepdims=True)
    xc = x - mu
    var = jnp.mean(xc * xc, axis=-1, keepdims=True)
    return xc * lax.rsqrt(var + EPS) * g + b


def _dot(a, b):
    return jnp.dot(a, b, preferred_element_type=F32)


def _dot_nt(a, b):
    return lax.dot_general(a, b, (((1,), (1,)), ((), ())), preferred_element_type=F32)


def _memkv_kernel(mem_ref, g_ref, wk_ref, wv_ref, k_ref, v_ref):
    mn = _rms(mem_ref[...], g_ref[...]).astype(BF16)
    k_ref[...] = _dot(mn, wk_ref[...])
    v_ref[...] = _dot(mn, wv_ref[...])


def _memkv(mem, g, wk, wv):
    r, d = mem.shape
    depth, _, dm = wk.shape
    tm = _tile(r, 512)
    return pl.pallas_call(
        _memkv_kernel,
        grid=(depth, r // tm),
        in_specs=[
            pl.BlockSpec((tm, d), lambda l, i: (i, 0)),
            pl.BlockSpec((None, 1, d), lambda l, i: (l, 0, 0)),
            pl.BlockSpec((None, d, dm), lambda l, i: (l, 0, 0)),
            pl.BlockSpec((None, d, dm), lambda l, i: (l, 0, 0)),
        ],
        out_specs=[
            pl.BlockSpec((None, tm, dm), lambda l, i: (l, i, 0)),
            pl.BlockSpec((None, tm, dm), lambda l, i: (l, i, 0)),
        ],
        out_shape=[jax.ShapeDtypeStruct((depth, r, dm), F32)] * 2,
        compiler_params=_params("arbitrary", "arbitrary"),
        name="memkv",
    )(mem, g, wk, wv)


def _inproj_kernel(x_ref, g_ref, w_ref, lng_ref, lnb_ref,
                   ua_ref, vn_ref, glu_ref, qm_ref, ga_ref, gb_ref, gm_ref, xn_sc, val_sc):
    j = pl.program_id(1)

    @pl.when(j == 0)
    def _():
        xn_sc[...] = _rms(x_ref[...], g_ref[...]).astype(BF16)

    h = _dot(xn_sc[...], w_ref[...])

    @pl.when(j == 0)
    def _():
        ua_ref[...] = _gelu(h).astype(BF16)

    @pl.when(j == 1)
    def _():
        vn_ref[...] = _layernorm(_gelu(h), lng_ref[...], lnb_ref[...])

    @pl.when(j == 2)
    def _():
        val_sc[...] = h

    @pl.when(j == 3)
    def _():
        glu_ref[...] = val_sc[...] * _sigmoid(h)

    @pl.when(j == 4)
    def _():
        qm_ref[...] = h.astype(BF16)

    @pl.when(j == 5)
    def _():
        ga_ref[...] = _sigmoid(h)

    @pl.when(j == 6)
    def _():
        gb_ref[...] = _sigmoid(h)

    @pl.when(j == 7)
    def _():
        gm_ref[...] = _sigmoid(h)


def _inproj(x, g, w_in, lng, lnb, layer):
    n, d = x.shape
    tm = _tile(n, 256)
    nseg = w_in.shape[2] // d
    assert nseg == 8
    tok = pl.BlockSpec((tm, d), lambda i, j: (i, 0))
    vec = pl.BlockSpec((None, 1, d), lambda i, j: (layer, 0, 0))
    return pl.pallas_call(
        _inproj_kernel,
        grid=(n // tm, nseg),
        in_specs=[tok, vec, pl.BlockSpec((None, d, d), lambda i, j: (layer, 0, j)), vec, vec],
        out_specs=[tok] * 7,
        out_shape=[jax.ShapeDtypeStruct((n, d), dt) for dt in (BF16, F32, F32, BF16, F32, F32, F32)],
        scratch_shapes=[pltpu.VMEM((tm, d), BF16), pltpu.VMEM((tm, d), F32)],
        compiler_params=_params("arbitrary", "arbitrary"),
        name="inproj",
    )(x, g, w_in, lng, lnb)


def _sgu_kernel(ua_ref, vn_ref, w_ref, b_ref, ya_ref, *, n_prompt_chunks, sample_shift):
    c = pl.program_id(0)
    row = lax.broadcasted_iota(jnp.int32, (CHUNK, CHUNK), 0)
    col = lax.broadcasted_iota(jnp.int32, (CHUNK, CHUNK), 1)
    same_seq = lax.shift_right_logical(row, sample_shift) == lax.shift_right_logical(col, sample_shift)
    keep = (col <= row) & (same_seq | (c < n_prompt_chunks))
    for grp in range(A_GROUPS):
        sl = slice(grp * LANES, (grp + 1) * LANES)
        w = jnp.where(keep, w_ref[grp], 0.0).astype(BF16)
        s = _dot(w, vn_ref[:, sl].astype(BF16)) + b_ref[:, sl]
        ya_ref[:, sl] = (ua_ref[:, sl].astype(F32) * s).astype(BF16)


def _sgu(ua, vn, w_eff, b_eff, n_prompt, dec_seq):
    n, d = ua.shape
    assert n % CHUNK == 0 and n_prompt % CHUNK == 0 and CHUNK % dec_seq == 0
    shift = int(math.log2(dec_seq))
    assert 1 << shift == dec_seq
    npc = n_prompt // CHUNK
    tok = pl.BlockSpec((CHUNK, d), lambda c: (c, 0))
    sel = lambda c: jnp.where(c >= npc, 1, 0)
    return pl.pallas_call(
        functools.partial(_sgu_kernel, n_prompt_chunks=npc, sample_shift=shift),
        grid=(n // CHUNK,),
        in_specs=[tok, tok,
                  pl.BlockSpec((None, A_GROUPS, CHUNK, CHUNK), lambda c: (sel(c), 0, 0, 0)),
                  pl.BlockSpec((None, CHUNK, d), lambda c: (sel(c), 0, 0))],
        out_specs=tok,
        out_shape=jax.ShapeDtypeStruct((n, d), BF16),
        compiler_params=_params("arbitrary"),
        name="sgu",
    )(ua, vn, w_eff, b_eff)


def _conv_prompt_kernel(cur_ref, halo_ref, w_ref, cb_ref, g_ref, b_ref, yb_ref, ext_sc, c_sc):
    t = pl.program_id(1)
    tc, d = cur_ref.shape
    halo = halo_ref[...]
    ext_sc[0:CONV_HALO, :] = jnp.where(t == 0, 0.0, halo)
    ext_sc[CONV_HALO:, :] = cur_ref[...]
    off = CONV_HALO - (CONV_W - 1)
    for cb in range(d // LANES):
        sl = slice(cb * LANES, (cb + 1) * LANES)
        acc = jnp.zeros((tc, LANES), F32)
        for k in range(CONV_W):
            acc = acc + ext_sc[off + k:off + k + tc, sl] * w_ref[k:k + 1, sl]
        c_sc[:, sl] = acc
    y = _layernorm(c_sc[...] + cb_ref[...], g_ref[...], b_ref[...])
    yb_ref[...] = (y * _sigmoid(y)).astype(BF16)


def _conv_prompt(glu, w, cb, g, b, layer, batch, seq):
    d = glu.shape[1]
    tc = _tile(seq, 256)
    nt = seq // tc
    hpt = tc // CONV_HALO
    vec = pl.BlockSpec((None, 1, d), lambda n, t: (layer, 0, 0))
    return pl.pallas_call(
        _conv_prompt_kernel,
        grid=(batch, nt),
        in_specs=[
            pl.BlockSpec((tc, d), lambda n, t: (n * nt + t, 0)),
            pl.BlockSpec((CONV_HALO, d), lambda n, t: (jnp.maximum((n * nt + t) * hpt - 1, 0), 0)),
            pl.BlockSpec((None, CONV_HALO, d), lambda n, t: (layer, 0, 0)),
            vec, vec, vec,
        ],
        out_specs=pl.BlockSpec((tc, d), lambda n, t: (n * nt + t, 0)),
        out_shape=jax.ShapeDtypeStruct((batch * seq, d), BF16),
        scratch_shapes=[pltpu.VMEM((CONV_HALO + tc, d), F32), pltpu.VMEM((tc, d), F32)],
        compiler_params=_params("arbitrary", "arbitrary"),
        name="conv_prompt",
    )(glu, glu, w, cb, g, b)


def _conv_sample_kernel(ext_ref, w_ref, cb_ref, g_ref, b_ref, yb_ref):
    t_new = yb_ref.shape[0]
    for t in range(t_new):
        acc = jnp.zeros(ext_ref.shape[1:], F32)
        for k in range(CONV_W):
            acc = acc + ext_ref[t + k] * w_ref[k:k + 1, :]
        y = _layernorm(acc + cb_ref[...], g_ref[...], b_ref[...])
        yb_ref[t] = (y * _sigmoid(y)).astype(BF16)


def _conv_sample(ext_t, w, cb, g, b, layer):
    te, s, d = ext_t.shape
    t_new = te - (CONV_W - 1)
    sb = 32 if s % 32 == 0 else s
    vec = pl.BlockSpec((None, 1, d), lambda i: (layer, 0, 0))
    return pl.pallas_call(
        _conv_sample_kernel,
        grid=(s // sb,),
        in_specs=[pl.BlockSpec((te, sb, d), lambda i: (0, i, 0)),
                  pl.BlockSpec((None, CONV_HALO, d), lambda i: (layer, 0, 0)),
                  vec, vec, vec],
        out_specs=pl.BlockSpec((t_new, sb, d), lambda i: (0, i, 0)),
        out_shape=jax.ShapeDtypeStruct((t_new, s, d), BF16),
        compiler_params=_params("arbitrary"),
        name="conv_sample",
    )(ext_t, w, cb, g, b)


def _attend(q, k, v, hd):
    outs = []
    for h in range(q.shape[1] // hd):
        sl = slice(h * hd, (h + 1) * hd)
        s = _dot_nt(q[:, sl], k[:, sl].astype(BF16)) * (hd ** -0.5)
        s = s - jnp.max(s, axis=-1, keepdims=True)
        p = jnp.exp(s)
        p = p / jnp.sum(p, axis=-1, keepdims=True)
        outs.append(_dot(p.astype(BF16), v[:, sl].astype(BF16)))
    return outs


def _attn_prompt_kernel(q_ref, k_ref, v_ref, ym_ref, *, hd):
    outs = _attend(q_ref[...], k_ref[...], v_ref[...], hd)
    for h, o in enumerate(outs):
        ym_ref[:, h * hd:(h + 1) * hd] = o.astype(BF16)


def _attn_prompt(qm, k, v, layer, batch, seq):
    dm = qm.shape[1]
    n_mem = k.shape[1] // batch
    tq = _tile(seq, 512)
    nt = seq // tq
    kv = pl.BlockSpec((None, n_mem, dm), lambda n, t: (layer, n, 0))
    return pl.pallas_call(
        functools.partial(_attn_prompt_kernel, hd=dm // M_HEADS),
        grid=(batch, nt),
        in_specs=[pl.BlockSpec((tq, dm), lambda n, t: (n * nt + t, 0)), kv, kv],
        out_specs=pl.BlockSpec((tq, dm), lambda n, t: (n * nt + t, 0)),
        out_shape=jax.ShapeDtypeStruct((batch * seq, dm), BF16),
        compiler_params=_params("arbitrary", "arbitrary"),
        name="attn_prompt",
    )(qm, k, v)


def _attn_sample_kernel(q_ref, k_ref, v_ref, ym_ref, *, hd, dec_seq):
    rows = q_ref.shape[0]
    q = q_ref[...]
    seq_of_row = lax.shift_right_logical(lax.broadcasted_iota(jnp.int32, (rows, hd), 0), int(math.log2(dec_seq)))
    acc = [jnp.zeros((rows, hd), F32) for _ in range(q.shape[1] // hd)]
    for s in range(rows // dec_seq):
        outs = _attend(q, k_ref[s], v_ref[s], hd)
        acc = [jnp.where(seq_of_row == s, o, a) for o, a in zip(outs, acc)]
    for h, a in enumerate(acc):
        ym_ref[:, h * hd:(h + 1) * hd] = a.astype(BF16)


def _attn_sample(qm, k, v, layer, row0, dec_batch, dec_seq):
    dm = qm.shape[1]
    n_mem = k.shape[2]
    rows = 16
    spb = rows // dec_seq
    assert rows % dec_seq == 0 and dec_batch % spb == 0 and row0 % rows == 0
    kv = pl.BlockSpec((None, spb, n_mem, dm), lambda i: (layer, i, 0, 0))
    return pl.pallas_call(
        functools.partial(_attn_sample_kernel, hd=dm // M_HEADS, dec_seq=dec_seq),
        grid=(dec_batch // spb,),
        in_specs=[pl.BlockSpec((rows, dm), lambda i: (row0 // rows + i, 0)), kv, kv],
        out_specs=pl.BlockSpec((rows, dm), lambda i: (i, 0)),
        out_shape=jax.ShapeDtypeStruct((dec_batch * dec_seq, dm), BF16),
        compiler_params=_params("arbitrary"),
        name="attn_sample",
    )(qm, k, v)


def _merge_kernel(x_ref, ya_ref, yb_ref, ym_ref, ga_ref, gb_ref, gm_ref,
                  woa_ref, wob_ref, wom_ref, wout_ref, g_ref, xo_ref, xnt_ref):
    merged = (ga_ref[...] * _dot(ya_ref[...], woa_ref[...])
              + gb_ref[...] * _dot(yb_ref[...], wob_ref[...])
              + gm_ref[...] * _dot(ym_ref[...], wom_ref[...]))
    x = x_ref[...] + _dot(merged.astype(BF16), wout_ref[...])
    xo_ref[...] = x
    xnt_ref[...] = _rms(x, g_ref[...]).T.astype(BF16)


def _merge(x, ya, yb, ym, ga, gb, gm, woa, wob, wom, wout, g, layer):
    n, d = x.shape
    tm = _tile(n, 256)
    tok = pl.BlockSpec((tm, d), lambda i: (i, 0))
    wgt = pl.BlockSpec((None, d, d), lambda i: (layer, 0, 0))
    return pl.pallas_call(
        _merge_kernel,
        grid=(n // tm,),
        in_specs=[tok] * 7 + [wgt] * 4 + [pl.BlockSpec((None, 1, d), lambda i: (layer, 0, 0))],
        out_specs=[tok, pl.BlockSpec((d, tm), lambda i: (0, i))],
        out_shape=[jax.ShapeDtypeStruct((n, d), F32), jax.ShapeDtypeStruct((d, n), BF16)],
        compiler_params=_params("arbitrary"),
        name="merge",
    )(x, ya, yb, ym, ga, gb, gm, woa, wob, wom, wout, g)


def _topk_rows(s, vals_sc):
    idx = lax.broadcasted_iota(jnp.int32, s.shape, 0).astype(F32)
    cur = s
    rank = jnp.full(s.shape, float(PEER_TOPK), F32)
    for r in range(PEER_TOPK):
        m = jnp.max(cur, axis=0, keepdims=True)
        first = jnp.min(jnp.where(cur == m, idx, float(s.shape[0])), axis=0, keepdims=True)
        sel = idx == first
        rank = jnp.where(sel, float(r), rank)
        cur = jnp.where(sel, NEG_INF, cur)
        vals_sc[r:r + 1, :] = m
    return rank


def _peer_route_kernel(xnt_ref, wqt_ref, keys_ref, r2_ref, a2_ref, bi_ref, c1_ref, v1_sc, v2_sc, cand_sc):
    qt = _dot(wqt_ref[...], xnt_ref[...])
    tl = qt.shape[1]
    cand_idx = lax.broadcasted_iota(jnp.int32, (_CAND_ROWS, tl), 0).astype(F32)
    cand_sc[...] = jnp.full((_CAND_ROWS, tl), NEG_INF, F32)
    for h in range(PEER_HEADS):
        s1 = _dot(keys_ref[2 * h], qt[(2 * h) * PEER_HALF:(2 * h + 1) * PEER_HALF, :].astype(BF16))
        s2 = _dot(keys_ref[2 * h + 1], qt[(2 * h + 1) * PEER_HALF:(2 * h + 2) * PEER_HALF, :].astype(BF16))
        rank1 = _topk_rows(s1, v1_sc)
        rank2 = _topk_rows(s2, v2_sc)
        for p, (a, b) in enumerate(_CAND):
            cand_sc[p:p + 1, :] = v1_sc[a:a + 1, :] + v2_sc[b:b + 1, :]
        cand = cand_sc[...]
        cur = cand
        for _ in range(PEER_TOPK):
            m = jnp.max(cur, axis=0, keepdims=True)
            first = jnp.min(jnp.where(cur == m, cand_idx, float(_CAND_ROWS)), axis=0, keepdims=True)
            cur = jnp.where(cand_idx == first, NEG_INF, cur)
        taken = (cur == NEG_INF) & (cand_idx < float(len(_CAND)))
        top = v1_sc[0:1, :] + v2_sc[0:1, :]
        z = jnp.sum(jnp.where(taken, jnp.exp(cand - top), 0.0), axis=0, keepdims=True)
        taken_f = jnp.where(taken, 1.0, 0.0)
        bi = jnp.zeros_like(s1)
        row0 = 0
        for a in range(PEER_TOPK):
            n_a = sum(1 for (aa, _) in _CAND if aa == a)
            in_a = (cand_idx >= float(row0)) & (cand_idx < float(row0 + n_a))
            count_a = jnp.sum(jnp.where(in_a, taken_f, 0.0), axis=0, keepdims=True)
            bi = jnp.where(rank1 == float(a), count_a, bi)
            row0 += n_a
        r2_ref[h] = rank2
        a2_ref[h] = jnp.exp(s2 - v2_sc[0:1, :])
        bi_ref[h] = bi
        c1_ref[h] = jnp.exp(s1 - v1_sc[0:1, :]) / z


def _peer_route(xnt, wqt, keys, layer):
    d, n = xnt.shape
    tl = _tile(n, 256)
    out = pl.BlockSpec((PEER_HEADS, N_KEYS, tl), lambda i: (0, 0, i))
    return pl.pallas_call(
        _peer_route_kernel,
        grid=(n // tl,),
        in_specs=[pl.BlockSpec((d, tl), lambda i: (0, i)),
                  pl.BlockSpec((None, d, d), lambda i: (layer, 0, 0)),
                  pl.BlockSpec((None, 2 * PEER_HEADS, N_KEYS, PEER_HALF), lambda i: (layer, 0, 0, 0))],
        out_specs=[out] * 4,
        out_shape=[jax.ShapeDtypeStruct((PEER_HEADS, N_KEYS, n), F32)] * 4,
        scratch_shapes=[pltpu.VMEM((PEER_TOPK, tl), F32), pltpu.VMEM((PEER_TOPK, tl), F32),
                        pltpu.VMEM((_CAND_ROWS, tl), F32)],
        compiler_params=_params("arbitrary"),
        name="peer_route",
    )(xnt, wqt, keys)


def _peer_dense_kernel(xnt_ref, u_ref, vt_ref, r2_ref, a2_ref, bi_ref, c1_ref, x_ref, g_ref, xo_ref, acc_sc, p_sc,
                       *, final_norm):
    e = pl.program_id(1)
    eb = u_ref.shape[0]

    @pl.when(e == 0)
    def _():
        acc_sc[...] = jnp.zeros_like(acc_sc)

    hid = _gelu(_dot(u_ref[...], xnt_ref[...]))
    for sub in range(eb // N_KEYS):
        e1 = e * (eb // N_KEYS) + sub
        w = jnp.zeros((N_KEYS, hid.shape[1]), F32)
        for h in range(PEER_HEADS):
            count = bi_ref[h, pl.ds(e1, 1), :]
            coef = c1_ref[h, pl.ds(e1, 1), :]
            w = w + jnp.where(r2_ref[h] < count, a2_ref[h] * coef, 0.0)
        p_sc[sub * N_KEYS:(sub + 1) * N_KEYS, :] = (w * hid[sub * N_KEYS:(sub + 1) * N_KEYS, :]).astype(BF16)
    acc_sc[...] += _dot(vt_ref[...], p_sc[...])

    @pl.when(e == pl.num_programs(1) - 1)
    def _():
        x = x_ref[...] + acc_sc[...].T
        xo_ref[...] = _rms(x, g_ref[...]) if final_norm else x


def _peer_dense(xnt, u, vt, r2, a2, bi, c1, x, g, layer, final_norm):
    d, n = xnt.shape
    n_exp = u.shape[1]
    tm = _tile(n, 512)
    eb = 512
    assert n_exp % eb == 0 and n_exp == N_KEYS * N_KEYS
    tab = pl.BlockSpec((PEER_HEADS, N_KEYS, tm), lambda i, e: (0, 0, i))
    tok = pl.BlockSpec((tm, d), lambda i, e: (i, 0))
    return pl.pallas_call(
        functools.partial(_peer_dense_kernel, final_norm=final_norm),
        grid=(n // tm, n_exp // eb),
        in_specs=[pl.BlockSpec((d, tm), lambda i, e: (0, i)),
                  pl.BlockSpec((None, eb, d), lambda i, e: (layer, e, 0)),
                  pl.BlockSpec((None, d, eb), lambda i, e: (layer, 0, e)),
                  tab, tab, tab, tab, tok,
                  pl.BlockSpec((1, d), lambda i, e: (0, 0))],
        out_specs=tok,
        out_shape=jax.ShapeDtypeStruct((n, d), F32),
        scratch_shapes=[pltpu.VMEM((d, tm), F32), pltpu.VMEM((eb, tm), BF16)],
        compiler_params=_params("arbitrary", "arbitrary"),
        name="peer_dense",
    )(xnt, u, vt, r2, a2, bi, c1, x, g)


def kernel(x_prompt, x_sample, mem_prompt, cache_mem_k, cache_mem_v, state_conv, norm_mix_g, w_in, a_ln_g, a_ln_b, a_ws, a_bs, b_conv_w, b_conv_b, b_ln_g, b_ln_b, mem_norm_g, w_mem_k, w_mem_v, w_o_a, w_o_b, w_o_m, w_out, norm_ffn_g, peer_wq, peer_keys, peer_u, peer_v, final_norm_g):
    batch, seq, d = x_prompt.shape
    dec_batch, dec_seq, _ = x_sample.shape
    depth = w_in.shape[0]
    n_mem = mem_prompt.shape[1]
    n_p, n_s = batch * seq, dec_batch * dec_seq
    hd = d // M_HEADS

    vec = lambda a: a.reshape(depth, 1, -1)
    w_in_b = w_in.astype(BF16)
    wk_b, wv_b = w_mem_k.astype(BF16), w_mem_v.astype(BF16)
    woa_b, wob_b, wom_b, wout_b = (w.astype(BF16) for w in (w_o_a, w_o_b, w_o_m, w_out))
    wqt_b = jnp.swapaxes(peer_wq, 1, 2).astype(BF16)
    keys_b = peer_keys.reshape(depth, 2 * PEER_HEADS, N_KEYS, PEER_HALF).astype(BF16)
    u_b = peer_u.astype(BF16)
    vt_b = jnp.swapaxes(peer_v, 1, 2).astype(BF16)
    conv_w = jnp.pad(b_conv_w, ((0, 0), (0, CONV_HALO - CONV_W), (0, 0)))
    reps = CHUNK // dec_seq
    ws_eff = jnp.stack([a_ws, jnp.tile(a_ws[:, :, :dec_seq, :dec_seq], (1, 1, reps, reps))], axis=1)
    bs_rows = jnp.repeat(jnp.swapaxes(a_bs, 1, 2), d // A_GROUPS, axis=2)
    bs_eff = jnp.stack([bs_rows, jnp.tile(bs_rows[:, :dec_seq], (1, reps, 1))], axis=1)

    mk_p, mv_p = _memkv(mem_prompt.reshape(batch * n_mem, d), vec(mem_norm_g), wk_b, wv_b)
    cache_k = cache_mem_k.reshape(depth, dec_batch, n_mem, d)
    cache_v = cache_mem_v.reshape(depth, dec_batch, n_mem, d)

    x = jnp.concatenate([x_prompt.reshape(n_p, d), x_sample.reshape(n_s, d)], axis=0)
    conv_p, conv_s, chunk_v_s = [], [], []
    for l in range(depth):
        ua, vn, glu, qm, ga, gb, gm = _inproj(x, vec(norm_mix_g), w_in_b, vec(a_ln_g), vec(a_ln_b), l)
        ya = _sgu(ua, vn, ws_eff[l], bs_eff[l], n_p, dec_seq)
        glu_p = glu[:n_p].reshape(batch, seq, d)
        ext_s = jnp.concatenate([state_conv[l], glu[n_p:].reshape(dec_batch, dec_seq, d)], axis=1)
        yb_p = _conv_prompt(glu, conv_w, vec(b_conv_b), vec(b_ln_g), vec(b_ln_b), l, batch, seq)
        yb_s = _conv_sample(jnp.swapaxes(ext_s, 0, 1), conv_w, vec(b_conv_b), vec(b_ln_g), vec(b_ln_b), l)
        yb = jnp.concatenate([yb_p, jnp.swapaxes(yb_s, 0, 1).reshape(n_s, d)], axis=0)
        ym_p = _attn_prompt(qm, mk_p, mv_p, l, batch, seq)
        ym_s = _attn_sample(qm, cache_k, cache_v, l, n_p, dec_batch, dec_seq)
        ym = jnp.concatenate([ym_p, ym_s], axis=0)
        x, xnt = _merge(x, ya, yb, ym, ga, gb, gm, woa_b, wob_b, wom_b, wout_b, vec(norm_ffn_g), l)
        r2, a2, bi, c1 = _peer_route(xnt, wqt_b, keys_b, l)
        x = _peer_dense(xnt, u_b, vt_b, r2, a2, bi, c1, x, final_norm_g.reshape(1, d), l, l == depth - 1)
        conv_p.append(glu_p[:, seq - (CONV_W - 1):])
        conv_s.append(ext_s[:, dec_seq:])
        chunk_v_s.append(vn[n_p:].reshape(dec_batch, dec_seq, d))

    y_prompt = x[:n_p].reshape(batch, seq, d)
    y_sample = x[n_p:].reshape(dec_batch, dec_seq, d)
    new_mem_k = mk_p.reshape(depth, batch, n_mem, M_HEADS, hd)
    new_mem_v = mv_p.reshape(depth, batch, n_mem, M_HEADS, hd)
    return (y_prompt, y_sample, new_mem_k, new_mem_v,
            jnp.stack(conv_p), jnp.stack(conv_s), jnp.stack(chunk_v_s))
```

```python
import functools
import math

import jax
import jax.numpy as jnp
from jax import lax
from jax.experimental import pallas as pl
from jax.experimental.pallas import tpu as pltpu

F32 = jnp.float32
BF16 = jnp.bfloat16
EPS = 1e-6
LANES = 128
SUBLANES = 8
BF16_ROWS = 16
VMEM_LIMIT_BYTES = 56 * 1024 * 1024

CHUNK = 128
A_GROUPS = 8
CONV_W = 31
CONV_HALO = 32
M_HEADS = 4
PEER_HEADS = 8
N_KEYS = 128
PEER_HALF = 64
PEER_TOPK = 16
NEG_INF = float("-inf")

_CAND = [(a, b) for a in range(PEER_TOPK) for b in range(PEER_TOPK) if (a + 1) * (b + 1) <= PEER_TOPK]
_CAND_ROWS = -(-len(_CAND) // 8) * 8
_CAND_WIDTH = [sum(1 for (a, _) in _CAND if a == r) for r in range(PEER_TOPK)]


def _tile(n, target, quantum=LANES):
    best = None
    t = quantum
    while t <= min(n, target):
        if n % t == 0:
            best = t
        t += quantum
    assert best is not None, (n, target, quantum)
    return best


def _params(*sem):
    return pltpu.CompilerParams(dimension_semantics=sem, vmem_limit_bytes=VMEM_LIMIT_BYTES)


def _gelu(x):
    return 0.5 * x * (1.0 + lax.erf(x * (2.0 ** -0.5)))


def _sigmoid(x):
    return 1.0 / (1.0 + jnp.exp(-x))


def _rms(x, g):
    ms = jnp.mean(x * x, axis=-1, keepdims=True)
    return x * lax.rsqrt(ms + EPS) * g


def _layernorm(x, g, b):
    mu = jnp.mean(x, axis=-1, keepdims=True)
    xc = x - mu
    var = jnp.mean(xc * xc, axis=-1, keepdims=True)
    return xc * lax.rsqrt(var + EPS) * g + b


def _dot(a, b):
    return jnp.dot(a, b, preferred_element_type=F32)


def _dot_nt(a, b):
    return lax.dot_general(a, b, (((1,), (1,)), ((), ())), preferred_element_type=F32)


def _memkv_kernel(mem_ref, g_ref, wk_ref, wv_ref, k_ref, v_ref):
    mn = _rms(mem_ref[...], g_ref[...]).astype(BF16)
    k_ref[...] = _dot(mn, wk_ref[...])
    v_ref[...] = _dot(mn, wv_ref[...])


def _memkv(mem, g, wk, wv):
    r, d = mem.shape
    depth, _, dm = wk.shape
    tm = _tile(r, 512)
    return pl.pallas_call(
        _memkv_kernel,
        grid=(depth, r // tm),
        in_specs=[
            pl.BlockSpec((tm, d), lambda l, i: (i, 0)),
            pl.BlockSpec((None, 1, d), lambda l, i: (l, 0, 0)),
            pl.BlockSpec((None, d, dm), lambda l, i: (l, 0, 0)),
            pl.BlockSpec((None, d, dm), lambda l, i: (l, 0, 0)),
        ],
        out_specs=[
            pl.BlockSpec((None, tm, dm), lambda l, i: (l, i, 0)),
            pl.BlockSpec((None, tm, dm), lambda l, i: (l, i, 0)),
        ],
        out_shape=[jax.ShapeDtypeStruct((depth, r, dm), F32)] * 2,
        compiler_params=_params("arbitrary", "arbitrary"),
        name="memkv",
    )(mem, g, wk, wv)


def _inproj_kernel(x_ref, g_ref, w_ref, lng_ref, lnb_ref,
                   ua_ref, vn_ref, glu_ref, qm_ref, ga_ref, gb_ref, gm_ref, xn_sc, val_sc):
    j = pl.program_id(1)
    n_chunks = 2
    cw = w_ref.shape[1] // n_chunks

    @pl.when(j == 0)
    def _():
        xn_sc[...] = _rms(x_ref[...], g_ref[...]).astype(BF16)

    def project():
        xn = xn_sc[...]
        return [_dot(xn, w_ref[:, c * cw:(c + 1) * cw]) for c in range(n_chunks)]

    def segment(seg, out_ref, fn):
        @pl.when(j == seg)
        def _():
            for c, h in enumerate(project()):
                cs = slice(c * cw, (c + 1) * cw)
                out_ref[:, cs] = fn(h, cs).astype(out_ref.dtype)

    segment(0, ua_ref, lambda h, cs: _gelu(h))

    @pl.when(j == 1)
    def _():
        v = jnp.concatenate([_gelu(h) for h in project()], axis=1)
        vn_ref[...] = _layernorm(v, lng_ref[...], lnb_ref[...])

    segment(2, val_sc, lambda h, cs: h)
    segment(3, glu_ref, lambda h, cs: val_sc[:, cs] * _sigmoid(h))
    segment(4, qm_ref, lambda h, cs: h)
    segment(5, ga_ref, lambda h, cs: _sigmoid(h))
    segment(6, gb_ref, lambda h, cs: _sigmoid(h))
    segment(7, gm_ref, lambda h, cs: _sigmoid(h))


def _inproj(x, g, w_in, lng, lnb, layer):
    n, d = x.shape
    tm = _tile(n, 512)
    nseg = w_in.shape[2] // d
    assert nseg == 8
    tok = pl.BlockSpec((tm, d), lambda i, j: (i, 0))
    vec = pl.BlockSpec((None, 1, d), lambda i, j: (layer, 0, 0))
    return pl.pallas_call(
        _inproj_kernel,
        grid=(n // tm, nseg),
        in_specs=[tok, vec, pl.BlockSpec((None, d, d), lambda i, j: (layer, 0, j)), vec, vec],
        out_specs=[tok] * 7,
        out_shape=[jax.ShapeDtypeStruct((n, d), dt) for dt in (BF16, F32, F32, BF16, BF16, BF16, BF16)],
        scratch_shapes=[pltpu.VMEM((tm, d), BF16), pltpu.VMEM((tm, d), F32)],
        compiler_params=_params("arbitrary", "arbitrary"),
        name="inproj",
    )(x, g, w_in, lng, lnb)


def _sgu_kernel(ua_ref, vn_ref, w_ref, b_ref, ya_ref, *, n_prompt_chunks, sample_shift):
    c = pl.program_id(0)
    row = lax.broadcasted_iota(jnp.int32, (CHUNK, CHUNK), 0)
    col = lax.broadcasted_iota(jnp.int32, (CHUNK, CHUNK), 1)
    same_seq = lax.shift_right_logical(row, sample_shift) == lax.shift_right_logical(col, sample_shift)
    keep = (col <= row) & (same_seq | (c < n_prompt_chunks))
    for grp in range(A_GROUPS):
        sl = slice(grp * LANES, (grp + 1) * LANES)
        w = jnp.where(keep, w_ref[grp], 0.0).astype(BF16)
        s = _dot(w, vn_ref[:, sl].astype(BF16)) + b_ref[:, sl]
        ya_ref[:, sl] = (ua_ref[:, sl].astype(F32) * s).astype(BF16)


def _sgu(ua, vn, w_eff, b_eff, n_prompt, dec_seq):
    n, d = ua.shape
    assert n % CHUNK == 0 and n_prompt % CHUNK == 0 and CHUNK % dec_seq == 0
    shift = int(math.log2(dec_seq))
    assert 1 << shift == dec_seq
    npc = n_prompt // CHUNK
    tok = pl.BlockSpec((CHUNK, d), lambda c: (c, 0))
    sel = lambda c: jnp.where(c >= npc, 1, 0)
    return pl.pallas_call(
        functools.partial(_sgu_kernel, n_prompt_chunks=npc, sample_shift=shift),
        grid=(n // CHUNK,),
        in_specs=[tok, tok,
                  pl.BlockSpec((None, A_GROUPS, CHUNK, CHUNK), lambda c: (sel(c), 0, 0, 0)),
                  pl.BlockSpec((None, CHUNK, d), lambda c: (sel(c), 0, 0))],
        out_specs=tok,
        out_shape=jax.ShapeDtypeStruct((n, d), BF16),
        compiler_params=_params("arbitrary"),
        name="sgu",
    )(ua, vn, w_eff, b_eff)


def _conv_prompt_kernel(cur_ref, halo_ref, w_ref, cb_ref, g_ref, b_ref, yb_ref, ext_sc, c_sc):
    t = pl.program_id(1)
    tc, d = cur_ref.shape
    halo = halo_ref[...]
    ext_sc[0:CONV_HALO, :] = jnp.where(t == 0, 0.0, halo)
    ext_sc[CONV_HALO:, :] = cur_ref[...]
    off = CONV_HALO - (CONV_W - 1)
    for cb in range(d // LANES):
        sl = slice(cb * LANES, (cb + 1) * LANES)
        acc = jnp.zeros((tc, LANES), F32)
        for k in range(CONV_W):
            acc = acc + ext_sc[off + k:off + k + tc, sl] * w_ref[k:k + 1, sl]
        c_sc[:, sl] = acc
    y = _layernorm(c_sc[...] + cb_ref[...], g_ref[...], b_ref[...])
    yb_ref[...] = (y * _sigmoid(y)).astype(BF16)


def _conv_prompt(glu, w, cb, g, b, layer, batch, seq):
    d = glu.shape[1]
    tc = _tile(seq, 256)
    nt = seq // tc
    hpt = tc // CONV_HALO
    vec = pl.BlockSpec((None, 1, d), lambda n, t: (layer, 0, 0))
    return pl.pallas_call(
        _conv_prompt_kernel,
        grid=(batch, nt),
        in_specs=[
            pl.BlockSpec((tc, d), lambda n, t: (n * nt + t, 0)),
            pl.BlockSpec((CONV_HALO, d), lambda n, t: (jnp.maximum((n * nt + t) * hpt - 1, 0), 0)),
            pl.BlockSpec((None, CONV_HALO, d), lambda n, t: (layer, 0, 0)),
            vec, vec, vec,
        ],
        out_specs=pl.BlockSpec((tc, d), lambda n, t: (n * nt + t, 0)),
        out_shape=jax.ShapeDtypeStruct((batch * seq, d), BF16),
        scratch_shapes=[pltpu.VMEM((CONV_HALO + tc, d), F32), pltpu.VMEM((tc, d), F32)],
        compiler_params=_params("arbitrary", "arbitrary"),
        name="conv_prompt",
    )(glu, glu, w, cb, g, b)


def _conv_sample_kernel(ext_ref, w_ref, cb_ref, g_ref, b_ref, yb_ref):
    t_new = yb_ref.shape[0]
    for t in range(t_new):
        acc = jnp.zeros(ext_ref.shape[1:], F32)
        for k in range(CONV_W):
            acc = acc + ext_ref[t + k] * w_ref[k:k + 1, :]
        y = _layernorm(acc + cb_ref[...], g_ref[...], b_ref[...])
        yb_ref[t] = (y * _sigmoid(y)).astype(BF16)


def _conv_sample(ext_t, w, cb, g, b, layer):
    te, s, d = ext_t.shape
    t_new = te - (CONV_W - 1)
    sb = 32 if s % 32 == 0 else s
    vec = pl.BlockSpec((None, 1, d), lambda i: (layer, 0, 0))
    return pl.pallas_call(
        _conv_sample_kernel,
        grid=(s // sb,),
        in_specs=[pl.BlockSpec((te, sb, d), lambda i: (0, i, 0)),
                  pl.BlockSpec((None, CONV_HALO, d), lambda i: (layer, 0, 0)),
                  vec, vec, vec],
        out_specs=pl.BlockSpec((t_new, sb, d), lambda i: (0, i, 0)),
        out_shape=jax.ShapeDtypeStruct((t_new, s, d), BF16),
        compiler_params=_params("arbitrary"),
        name="conv_sample",
    )(ext_t, w, cb, g, b)


def _attend(q, k, v, hd):
    outs = []
    for h in range(q.shape[1] // hd):
        sl = slice(h * hd, (h + 1) * hd)
        s = _dot_nt(q[:, sl], k[:, sl].astype(BF16)) * (hd ** -0.5)
        s = s - jnp.max(s, axis=-1, keepdims=True)
        p = jnp.exp(s)
        p = p / jnp.sum(p, axis=-1, keepdims=True)
        outs.append(_dot(p.astype(BF16), v[:, sl].astype(BF16)))
    return outs


def _attn_prompt_kernel(q_ref, k_ref, v_ref, ym_ref, *, hd):
    outs = _attend(q_ref[...], k_ref[...], v_ref[...], hd)
    for h, o in enumerate(outs):
        ym_ref[:, h * hd:(h + 1) * hd] = o.astype(BF16)


def _attn_prompt(qm, k, v, layer, batch, seq):
    dm = qm.shape[1]
    n_mem = k.shape[1] // batch
    tq = _tile(seq, 512)
    nt = seq // tq
    kv = pl.BlockSpec((None, n_mem, dm), lambda n, t: (layer, n, 0))
    return pl.pallas_call(
        functools.partial(_attn_prompt_kernel, hd=dm // M_HEADS),
        grid=(batch, nt),
        in_specs=[pl.BlockSpec((tq, dm), lambda n, t: (n * nt + t, 0)), kv, kv],
        out_specs=pl.BlockSpec((tq, dm), lambda n, t: (n * nt + t, 0)),
        out_shape=jax.ShapeDtypeStruct((batch * seq, dm), BF16),
        compiler_params=_params("arbitrary", "arbitrary"),
        name="attn_prompt",
    )(qm, k, v)


def _attn_sample_kernel(q_ref, k_ref, v_ref, ym_ref, *, hd, dec_seq):
    rows = q_ref.shape[0]
    q = q_ref[...]
    seq_of_row = lax.shift_right_logical(lax.broadcasted_iota(jnp.int32, (rows, hd), 0), int(math.log2(dec_seq)))
    acc = [jnp.zeros((rows, hd), F32) for _ in range(q.shape[1] // hd)]
    for s in range(rows // dec_seq):
        outs = _attend(q, k_ref[s], v_ref[s], hd)
        acc = [jnp.where(seq_of_row == s, o, a) for o, a in zip(outs, acc)]
    for h, a in enumerate(acc):
        ym_ref[:, h * hd:(h + 1) * hd] = a.astype(BF16)


def _attn_sample(qm, k, v, layer, row0, dec_batch, dec_seq):
    dm = qm.shape[1]
    n_mem = k.shape[2]
    rows = 16
    spb = rows // dec_seq
    assert rows % dec_seq == 0 and dec_batch % spb == 0 and row0 % rows == 0
    kv = pl.BlockSpec((None, spb, n_mem, dm), lambda i: (layer, i, 0, 0))
    return pl.pallas_call(
        functools.partial(_attn_sample_kernel, hd=dm // M_HEADS, dec_seq=dec_seq),
        grid=(dec_batch // spb,),
        in_specs=[pl.BlockSpec((rows, dm), lambda i: (row0 // rows + i, 0)), kv, kv],
        out_specs=pl.BlockSpec((rows, dm), lambda i: (i, 0)),
        out_shape=jax.ShapeDtypeStruct((dec_batch * dec_seq, dm), BF16),
        compiler_params=_params("arbitrary"),
        name="attn_sample",
    )(qm, k, v)


def _merge_kernel(x_ref, ya_ref, yb_ref, ym_ref, ga_ref, gb_ref, gm_ref,
                  woa_ref, wob_ref, wom_ref, wout_ref, g_ref, xo_ref, xnt_ref):
    merged = (ga_ref[...] * _dot(ya_ref[...], woa_ref[...])
              + gb_ref[...] * _dot(yb_ref[...], wob_ref[...])
              + gm_ref[...] * _dot(ym_ref[...], wom_ref[...]))
    x = x_ref[...] + _dot(merged.astype(BF16), wout_ref[...])
    xo_ref[...] = x
    xnt_ref[...] = _rms(x, g_ref[...]).T.astype(BF16)


def _merge(x, ya, yb, ym, ga, gb, gm, woa, wob, wom, wout, g, layer):
    n, d = x.shape
    tm = _tile(n, 256)
    tok = pl.BlockSpec((tm, d), lambda i: (i, 0))
    wgt = pl.BlockSpec((None, d, d), lambda i: (layer, 0, 0))
    return pl.pallas_call(
        _merge_kernel,
        grid=(n // tm,),
        in_specs=[tok] * 7 + [wgt] * 4 + [pl.BlockSpec((None, 1, d), lambda i: (layer, 0, 0))],
        out_specs=[tok, pl.BlockSpec((d, tm), lambda i: (0, i))],
        out_shape=[jax.ShapeDtypeStruct((n, d), F32), jax.ShapeDtypeStruct((d, n), BF16)],
        compiler_params=_params("arbitrary"),
        name="merge",
    )(x, ya, yb, ym, ga, gb, gm, woa, wob, wom, wout, g)


def _extract_topk(s, vals_sc, exact):
    idx = lax.broadcasted_iota(jnp.int32, s.shape, 0).astype(F32)
    cur = s
    rank = jnp.full(s.shape, float(PEER_TOPK), F32)
    for r in range(PEER_TOPK):
        m = jnp.max(cur, axis=0, keepdims=True)
        if exact:
            first = jnp.min(jnp.where(cur == m, idx, float(s.shape[0])), axis=0, keepdims=True)
            sel = idx == first
        else:
            sel = cur == m
        rank = jnp.where(sel, float(r), rank)
        cur = jnp.where(sel, NEG_INF, cur)
        vals_sc[r:r + 1, :] = m
    return rank


def _take_candidates(cand, cand_idx, exact):
    cur = cand
    for _ in range(PEER_TOPK):
        m = jnp.max(cur, axis=0, keepdims=True)
        if exact:
            first = jnp.min(jnp.where(cur == m, cand_idx, float(_CAND_ROWS)), axis=0, keepdims=True)
            cur = jnp.where(cand_idx == first, NEG_INF, cur)
        else:
            cur = jnp.where(cur == m, NEG_INF, cur)
    return jnp.where((cur == NEG_INF) & (cand_idx < float(len(_CAND))), 1.0, 0.0)


def _peer_route_kernel(xnt_ref, wqt_ref, keys_ref, r2_ref, a2_ref, bi_ref, c1_ref,
                       v1_sc, v2_sc, cand_sc, rank_sc, taken_sc, count_sc):
    qt = _dot(wqt_ref[...], xnt_ref[...])
    tl = qt.shape[1]
    cand_idx = lax.broadcasted_iota(jnp.int32, (_CAND_ROWS, tl), 0).astype(F32)
    cand_sc[...] = jnp.full((_CAND_ROWS, tl), NEG_INF, F32)

    def miscount(flags):
        return jnp.abs(jnp.sum(flags, axis=0, keepdims=True) - float(PEER_TOPK))

    for h in range(PEER_HEADS):
        s1 = _dot(keys_ref[2 * h], qt[(2 * h) * PEER_HALF:(2 * h + 1) * PEER_HALF, :].astype(BF16))
        s2 = _dot(keys_ref[2 * h + 1], qt[(2 * h + 1) * PEER_HALF:(2 * h + 2) * PEER_HALF, :].astype(BF16))

        def route(exact):
            rank_sc[0] = _extract_topk(s1, v1_sc, exact)
            rank_sc[1] = _extract_topk(s2, v2_sc, exact)
            for p, (a, b) in enumerate(_CAND):
                cand_sc[p:p + 1, :] = v1_sc[a:a + 1, :] + v2_sc[b:b + 1, :]
            taken = _take_candidates(cand_sc[...], cand_idx, exact)
            taken_sc[...] = taken
            if exact:
                row0 = 0
                for a, n_a in enumerate(_CAND_WIDTH):
                    in_a = (cand_idx >= float(row0)) & (cand_idx < float(row0 + n_a))
                    count_sc[a:a + 1, :] = jnp.sum(jnp.where(in_a, taken, 0.0), axis=0, keepdims=True)
                    row0 += n_a
            else:
                tau = jnp.min(jnp.where(taken > 0.0, cand_sc[...], float("inf")), axis=0, keepdims=True)
                v2 = v2_sc[...]
                b_idx = lax.broadcasted_iota(jnp.int32, v2.shape, 0)
                for a, n_a in enumerate(_CAND_WIDTH):
                    hit = (v1_sc[a:a + 1, :] + v2 >= tau) & (b_idx < n_a)
                    count_sc[a:a + 1, :] = jnp.sum(jnp.where(hit, 1.0, 0.0), axis=0, keepdims=True)

        route(False)
        ranked = lambda r: jnp.where(r < float(PEER_TOPK), 1.0, 0.0)
        tied = jnp.max(miscount(ranked(rank_sc[0])) + miscount(ranked(rank_sc[1])) + miscount(taken_sc[...]))

        @pl.when(tied > 0.0)
        def _():
            route(True)

        rank1, rank2, taken, cand = rank_sc[0], rank_sc[1], taken_sc[...], cand_sc[...]
        top = v1_sc[0:1, :] + v2_sc[0:1, :]
        z = jnp.sum(taken * jnp.exp(cand - top), axis=0, keepdims=True)
        bi = jnp.zeros_like(s1)
        for a in range(PEER_TOPK):
            bi = jnp.where(rank1 == float(a), count_sc[a:a + 1, :], bi)
        r2_ref[h] = rank2.astype(BF16)
        a2_ref[h] = jnp.exp(s2 - v2_sc[0:1, :]).astype(BF16)
        bi_ref[h] = bi
        c1_ref[h] = jnp.exp(s1 - v1_sc[0:1, :]) * (0.5 / z)


def _peer_route(xnt, wqt, keys, layer):
    d, n = xnt.shape
    tl = _tile(n, 256)
    rows = pl.BlockSpec((PEER_HEADS, N_KEYS, tl), lambda i: (0, 0, i))
    return pl.pallas_call(
        _peer_route_kernel,
        grid=(n // tl,),
        in_specs=[pl.BlockSpec((d, tl), lambda i: (0, i)),
                  pl.BlockSpec((None, d, d), lambda i: (layer, 0, 0)),
                  pl.BlockSpec((None, 2 * PEER_HEADS, N_KEYS, PEER_HALF), lambda i: (layer, 0, 0, 0))],
        out_specs=[rows] * 4,
        out_shape=[jax.ShapeDtypeStruct((PEER_HEADS, N_KEYS, n), dt) for dt in (BF16, BF16, F32, F32)],
        scratch_shapes=[pltpu.VMEM((PEER_TOPK, tl), F32), pltpu.VMEM((PEER_TOPK, tl), F32),
                        pltpu.VMEM((_CAND_ROWS, tl), F32), pltpu.VMEM((2, N_KEYS, tl), F32),
                        pltpu.VMEM((_CAND_ROWS, tl), F32), pltpu.VMEM((PEER_TOPK, tl), F32)],
        compiler_params=_params("arbitrary"),
        name="peer_route",
    )(xnt, wqt, keys)


def _peer_dense_kernel(xnt_ref, u_ref, vt_ref, r2_ref, a2_ref, bi_ref, c1_ref, x_ref, g_ref, xo_ref, acc_sc, p_sc,
                       *, final_norm, n_chunks):
    e = pl.program_id(1)
    n_eb = pl.num_programs(1) - 1
    eb, tm = u_ref.shape[0], xnt_ref.shape[1]
    ce = eb // n_chunks
    cur = lax.rem(e, 2)
    prev = 1 - cur
    zero = jnp.zeros((), BF16)

    def row_tile(ref, h, e1):
        row = jnp.broadcast_to(ref[h, pl.ds(e1, 1), :], (BF16_ROWS, tm)).astype(BF16)
        return jnp.concatenate([row] * (N_KEYS // BF16_ROWS), axis=0)

    @pl.when(e == 0)
    def _():
        acc_sc[...] = jnp.zeros_like(acc_sc)
        p_sc[1] = jnp.zeros(p_sc.shape[1:], BF16)

    @pl.when(e < n_eb)
    def _():
        xnt = xnt_ref[...]
        hids = [_dot(u_ref[c * ce:(c + 1) * ce, :], xnt) for c in range(n_chunks)]
        acc_sc[...] += _dot(vt_ref[...], p_sc[prev])
        for c in range(n_chunks):
            act = (hids[c] * (1.0 + lax.erf(hids[c] * (2.0 ** -0.5)))).astype(BF16)
            for sub in range(ce // N_KEYS):
                r0 = c * ce + sub * N_KEYS
                e1 = e * (eb // N_KEYS) + r0 // N_KEYS
                w = None
                for h in range(PEER_HEADS):
                    term = jnp.where(r2_ref[h] < row_tile(bi_ref, h, e1), a2_ref[h], zero) * row_tile(c1_ref, h, e1)
                    w = term if w is None else w + term
                p_sc[cur, r0:r0 + N_KEYS, :] = w * act[sub * N_KEYS:(sub + 1) * N_KEYS, :]

    @pl.when(e == n_eb)
    def _():
        x = x_ref[...] + (acc_sc[...] + _dot(vt_ref[...], p_sc[prev])).T
        xo_ref[...] = _rms(x, g_ref[...]) if final_norm else x


def _peer_dense(xnt, u, vt, r2, a2, bi, c1, x, g, layer, final_norm):
    d, n = xnt.shape
    n_exp = u.shape[1]
    tm = _tile(n, 512, 2 * LANES)
    eb, n_chunks = 1024, 2
    n_eb = n_exp // eb
    assert n_exp % eb == 0 and n_exp == N_KEYS * N_KEYS
    tab = pl.BlockSpec((PEER_HEADS, N_KEYS, tm), lambda i, e: (0, 0, i))
    tok = pl.BlockSpec((tm, d), lambda i, e: (i, 0))
    return pl.pallas_call(
        functools.partial(_peer_dense_kernel, final_norm=final_norm, n_chunks=n_chunks),
        grid=(n // tm, n_eb + 1),
        in_specs=[pl.BlockSpec((d, tm), lambda i, e: (0, i)),
                  pl.BlockSpec((None, eb, d), lambda i, e: (layer, jnp.minimum(e, n_eb - 1), 0)),
                  pl.BlockSpec((None, d, eb), lambda i, e: (layer, 0, jnp.maximum(e - 1, 0))),
                  tab, tab, tab, tab, tok,
                  pl.BlockSpec((1, d), lambda i, e: (0, 0))],
        out_specs=tok,
        out_shape=jax.ShapeDtypeStruct((n, d), F32),
        scratch_shapes=[pltpu.VMEM((d, tm), F32), pltpu.VMEM((2, eb, tm), BF16)],
        compiler_params=_params("arbitrary", "arbitrary"),
        name="peer_dense",
    )(xnt, u, vt, r2, a2, bi, c1, x, g)


def kernel(x_prompt, x_sample, mem_prompt, cache_mem_k, cache_mem_v, state_conv, norm_mix_g, w_in, a_ln_g, a_ln_b, a_ws, a_bs, b_conv_w, b_conv_b, b_ln_g, b_ln_b, mem_norm_g, w_mem_k, w_mem_v, w_o_a, w_o_b, w_o_m, w_out, norm_ffn_g, peer_wq, peer_keys, peer_u, peer_v, final_norm_g):
    batch, seq, d = x_prompt.shape
    dec_batch, dec_seq, _ = x_sample.shape
    depth = w_in.shape[0]
    n_mem = mem_prompt.shape[1]
    n_p, n_s = batch * seq, dec_batch * dec_seq
    hd = d // M_HEADS

    vec = lambda a: a.reshape(depth, 1, -1)
    w_in_b = w_in.astype(BF16)
    wk_b, wv_b = w_mem_k.astype(BF16), w_mem_v.astype(BF16)
    woa_b, wob_b, wom_b, wout_b = (w.astype(BF16) for w in (w_o_a, w_o_b, w_o_m, w_out))
    wqt_b = jnp.swapaxes(peer_wq, 1, 2).astype(BF16)
    keys_b = peer_keys.reshape(depth, 2 * PEER_HEADS, N_KEYS, PEER_HALF).astype(BF16)
    u_b = peer_u.astype(BF16)
    vt_b = jnp.swapaxes(peer_v, 1, 2).astype(BF16)
    conv_w = jnp.pad(b_conv_w, ((0, 0), (0, CONV_HALO - CONV_W), (0, 0)))
    reps = CHUNK // dec_seq
    ws_eff = jnp.stack([a_ws, jnp.tile(a_ws[:, :, :dec_seq, :dec_seq], (1, 1, reps, reps))], axis=1)
    bs_rows = jnp.repeat(jnp.swapaxes(a_bs, 1, 2), d // A_GROUPS, axis=2)
    bs_eff = jnp.stack([bs_rows, jnp.tile(bs_rows[:, :dec_seq], (1, reps, 1))], axis=1)

    mk_p, mv_p = _memkv(mem_prompt.reshape(batch * n_mem, d), vec(mem_norm_g), wk_b, wv_b)
    cache_k = cache_mem_k.reshape(depth, dec_batch, n_mem, d)
    cache_v = cache_mem_v.reshape(depth, dec_batch, n_mem, d)

    x = jnp.concatenate([x_prompt.reshape(n_p, d), x_sample.reshape(n_s, d)], axis=0)
    conv_p, conv_s, chunk_v_s = [], [], []
    for l in range(depth):
        ua, vn, glu, qm, ga, gb, gm = _inproj(x, vec(norm_mix_g), w_in_b, vec(a_ln_g), vec(a_ln_b), l)
        ya = _sgu(ua, vn, ws_eff[l], bs_eff[l], n_p, dec_seq)
        ext_s = jnp.concatenate([state_conv[l], glu[n_p:].reshape(dec_batch, dec_seq, d)], axis=1)
        yb_p = _conv_prompt(glu, conv_w, vec(b_conv_b), vec(b_ln_g), vec(b_ln_b), l, batch, seq)
        yb_s = _conv_sample(jnp.swapaxes(ext_s, 0, 1), conv_w, vec(b_conv_b), vec(b_ln_g), vec(b_ln_b), l)
        yb = jnp.concatenate([yb_p, jnp.swapaxes(yb_s, 0, 1).reshape(n_s, d)], axis=0)
        ym_p = _attn_prompt(qm, mk_p, mv_p, l, batch, seq)
        ym_s = _attn_sample(qm, cache_k, cache_v, l, n_p, dec_batch, dec_seq)
        ym = jnp.concatenate([ym_p, ym_s], axis=0)
        x, xnt = _merge(x, ya, yb, ym, ga, gb, gm, woa_b, wob_b, wom_b, wout_b, vec(norm_ffn_g), l)
        r2, a2, bi, c1 = _peer_route(xnt, wqt_b, keys_b, l)
        x = _peer_dense(xnt, u_b, vt_b, r2, a2, bi, c1, x, final_norm_g.reshape(1, d), l, l == depth - 1)
        conv_p.append(jnp.stack([glu[(b + 1) * seq - (CONV_W - 1):(b + 1) * seq] for b in range(batch)]))
        conv_s.append(ext_s[:, dec_seq:])
        chunk_v_s.append(vn[n_p:].reshape(dec_batch, dec_seq, d))

    y_prompt = x[:n_p].reshape(batch, seq, d)
    y_sample = x[n_p:].reshape(dec_batch, dec_seq, d)
    new_mem_k = mk_p.reshape(depth, batch, n_mem, M_HEADS, hd)
    new_mem_v = mv_p.reshape(depth, batch, n_mem, M_HEADS, hd)
    return (y_prompt, y_sample, new_mem_k, new_mem_v,
            jnp.stack(conv_p), jnp.stack(conv_s), jnp.stack(chunk_v_s))
```

```python
import functools
import math

import jax
import jax.numpy as jnp
from jax import lax
from jax.experimental import pallas as pl
from jax.experimental.pallas import tpu as pltpu

F32 = jnp.float32
BF16 = jnp.bfloat16
EPS = 1e-6
LANES = 128
SUBLANES = 8
BF16_ROWS = 16
VMEM_LIMIT_BYTES = 56 * 1024 * 1024

CHUNK = 128
A_GROUPS = 8
CONV_W = 31
CONV_HALO = 32
M_HEADS = 4
PEER_HEADS = 8
N_KEYS = 128
PEER_HALF = 64
PEER_TOPK = 16
NEG_INF = float("-inf")
_MARK_BASE = 2.0 ** 127
_MARK_STEP = 2.0 ** 122

_CAND = [(a, b) for a in range(PEER_TOPK) for b in range(PEER_TOPK) if (a + 1) * (b + 1) <= PEER_TOPK]
_CAND_ROWS = -(-len(_CAND) // 8) * 8
_CAND_WIDTH = [sum(1 for (a, _) in _CAND if a == r) for r in range(PEER_TOPK)]


def _tile(n, target, quantum=LANES):
    best = None
    t = quantum
    while t <= min(n, target):
        if n % t == 0:
            best = t
        t += quantum
    assert best is not None, (n, target, quantum)
    return best


def _params(*sem):
    return pltpu.CompilerParams(dimension_semantics=sem, vmem_limit_bytes=VMEM_LIMIT_BYTES)


def _gelu(x):
    return 0.5 * x * (1.0 + lax.erf(x * (2.0 ** -0.5)))


def _sigmoid(x):
    return 1.0 / (1.0 + jnp.exp(-x))


def _rms(x, g):
    ms = jnp.mean(x * x, axis=-1, keepdims=True)
    return x * lax.rsqrt(ms + EPS) * g


def _layernorm(x, g, b):
    mu = jnp.mean(x, axis=-1, keepdims=True)
    xc = x - mu
    var = jnp.mean(xc * xc, axis=-1, keepdims=True)
    return xc * lax.rsqrt(var + EPS) * g + b


def _dot(a, b):
    return jnp.dot(a, b, preferred_element_type=F32)


def _dot_nt(a, b):
    return lax.dot_general(a, b, (((1,), (1,)), ((), ())), preferred_element_type=F32)


def _memkv_kernel(mem_ref, g_ref, wk_ref, wv_ref, k_ref, v_ref):
    mn = _rms(mem_ref[...], g_ref[...]).astype(BF16)
    k_ref[...] = _dot(mn, wk_ref[...])
    v_ref[...] = _dot(mn, wv_ref[...])


def _memkv(mem, g, wk, wv):
    r, d = mem.shape
    depth, _, dm = wk.shape
    tm = _tile(r, 512)
    return pl.pallas_call(
        _memkv_kernel,
        grid=(depth, r // tm),
        in_specs=[
            pl.BlockSpec((tm, d), lambda l, i: (i, 0)),
            pl.BlockSpec((None, 1, d), lambda l, i: (l, 0, 0)),
            pl.BlockSpec((None, d, dm), lambda l, i: (l, 0, 0)),
            pl.BlockSpec((None, d, dm), lambda l, i: (l, 0, 0)),
        ],
        out_specs=[
            pl.BlockSpec((None, tm, dm), lambda l, i: (l, i, 0)),
            pl.BlockSpec((None, tm, dm), lambda l, i: (l, i, 0)),
        ],
        out_shape=[jax.ShapeDtypeStruct((depth, r, dm), F32)] * 2,
        compiler_params=_params("arbitrary", "arbitrary"),
        name="memkv",
    )(mem, g, wk, wv)


def _inproj_kernel(x_ref, g_ref, w_ref, lng_ref, lnb_ref,
                   ua_ref, vn_ref, glu_ref, qm_ref, ga_ref, gb_ref, gm_ref, xn_sc, val_sc):
    j = pl.program_id(1)
    n_chunks = 4
    cw = w_ref.shape[1] // n_chunks

    @pl.when(j == 0)
    def _():
        xn_sc[...] = _rms(x_ref[...], g_ref[...]).astype(BF16)

    def project():
        xn = xn_sc[...]
        return [_dot(xn, w_ref[:, c * cw:(c + 1) * cw]) for c in range(n_chunks)]

    def segment(seg, out_ref, fn):
        @pl.when(j == seg)
        def _():
            for c, h in enumerate(project()):
                cs = slice(c * cw, (c + 1) * cw)
                out_ref[:, cs] = fn(h, cs).astype(out_ref.dtype)

    segment(0, ua_ref, lambda h, cs: _gelu(h))

    @pl.when(j == 1)
    def _():
        v = jnp.concatenate([_gelu(h) for h in project()], axis=1)
        vn_ref[...] = _layernorm(v, lng_ref[...], lnb_ref[...])

    segment(2, val_sc, lambda h, cs: h)
    segment(3, glu_ref, lambda h, cs: val_sc[:, cs] * _sigmoid(h))
    segment(4, qm_ref, lambda h, cs: h)
    segment(5, ga_ref, lambda h, cs: _sigmoid(h))
    segment(6, gb_ref, lambda h, cs: _sigmoid(h))
    segment(7, gm_ref, lambda h, cs: _sigmoid(h))


def _inproj(x, g, w_in, lng, lnb, layer):
    n, d = x.shape
    tm = _tile(n, 512)
    nseg = w_in.shape[2] // d
    assert nseg == 8
    tok = pl.BlockSpec((tm, d), lambda i, j: (i, 0))
    vec = pl.BlockSpec((None, 1, d), lambda i, j: (layer, 0, 0))
    return pl.pallas_call(
        _inproj_kernel,
        grid=(n // tm, nseg),
        in_specs=[tok, vec, pl.BlockSpec((None, d, d), lambda i, j: (layer, 0, j)), vec, vec],
        out_specs=[tok] * 7,
        out_shape=[jax.ShapeDtypeStruct((n, d), dt) for dt in (BF16, F32, F32, BF16, BF16, BF16, BF16)],
        scratch_shapes=[pltpu.VMEM((tm, d), BF16), pltpu.VMEM((tm, d), F32)],
        compiler_params=_params("arbitrary", "arbitrary"),
        name="inproj",
    )(x, g, w_in, lng, lnb)


def _sgu_kernel(ua_ref, vn_ref, w_ref, b_ref, ya_ref, *, n_prompt_chunks, sample_shift):
    c = pl.program_id(0)
    row = lax.broadcasted_iota(jnp.int32, (CHUNK, CHUNK), 0)
    col = lax.broadcasted_iota(jnp.int32, (CHUNK, CHUNK), 1)
    same_seq = lax.shift_right_logical(row, sample_shift) == lax.shift_right_logical(col, sample_shift)
    keep = (col <= row) & (same_seq | (c < n_prompt_chunks))
    for grp in range(A_GROUPS):
        sl = slice(grp * LANES, (grp + 1) * LANES)
        w = jnp.where(keep, w_ref[grp], 0.0).astype(BF16)
        s = _dot(w, vn_ref[:, sl].astype(BF16)) + b_ref[:, sl]
        ya_ref[:, sl] = (ua_ref[:, sl].astype(F32) * s).astype(BF16)


def _sgu(ua, vn, w_eff, b_eff, n_prompt, dec_seq):
    n, d = ua.shape
    assert n % CHUNK == 0 and n_prompt % CHUNK == 0 and CHUNK % dec_seq == 0
    shift = int(math.log2(dec_seq))
    assert 1 << shift == dec_seq
    npc = n_prompt // CHUNK
    tok = pl.BlockSpec((CHUNK, d), lambda c: (c, 0))
    sel = lambda c: jnp.where(c >= npc, 1, 0)
    return pl.pallas_call(
        functools.partial(_sgu_kernel, n_prompt_chunks=npc, sample_shift=shift),
        grid=(n // CHUNK,),
        in_specs=[tok, tok,
                  pl.BlockSpec((None, A_GROUPS, CHUNK, CHUNK), lambda c: (sel(c), 0, 0, 0)),
                  pl.BlockSpec((None, CHUNK, d), lambda c: (sel(c), 0, 0))],
        out_specs=tok,
        out_shape=jax.ShapeDtypeStruct((n, d), BF16),
        compiler_params=_params("arbitrary"),
        name="sgu",
    )(ua, vn, w_eff, b_eff)


def _conv_prompt_kernel(cur_ref, halo_ref, w_ref, cb_ref, g_ref, b_ref, yb_ref, ext_sc, c_sc):
    t = pl.program_id(1)
    tc, d = cur_ref.shape
    halo = halo_ref[...]
    ext_sc[0:CONV_HALO, :] = jnp.where(t == 0, 0.0, halo)
    ext_sc[CONV_HALO:, :] = cur_ref[...]
    off = CONV_HALO - (CONV_W - 1)
    for cb in range(d // LANES):
        sl = slice(cb * LANES, (cb + 1) * LANES)
        acc = jnp.zeros((tc, LANES), F32)
        for k in range(CONV_W):
            acc = acc + ext_sc[off + k:off + k + tc, sl] * w_ref[k:k + 1, sl]
        c_sc[:, sl] = acc
    y = _layernorm(c_sc[...] + cb_ref[...], g_ref[...], b_ref[...])
    yb_ref[...] = (y * _sigmoid(y)).astype(BF16)


def _conv_prompt(glu, w, cb, g, b, layer, batch, seq):
    d = glu.shape[1]
    tc = _tile(seq, 256)
    nt = seq // tc
    hpt = tc // CONV_HALO
    vec = pl.BlockSpec((None, 1, d), lambda n, t: (layer, 0, 0))
    return pl.pallas_call(
        _conv_prompt_kernel,
        grid=(batch, nt),
        in_specs=[
            pl.BlockSpec((tc, d), lambda n, t: (n * nt + t, 0)),
            pl.BlockSpec((CONV_HALO, d), lambda n, t: (jnp.maximum((n * nt + t) * hpt - 1, 0), 0)),
            pl.BlockSpec((None, CONV_HALO, d), lambda n, t: (layer, 0, 0)),
            vec, vec, vec,
        ],
        out_specs=pl.BlockSpec((tc, d), lambda n, t: (n * nt + t, 0)),
        out_shape=jax.ShapeDtypeStruct((batch * seq, d), BF16),
        scratch_shapes=[pltpu.VMEM((CONV_HALO + tc, d), F32), pltpu.VMEM((tc, d), F32)],
        compiler_params=_params("arbitrary", "arbitrary"),
        name="conv_prompt",
    )(glu, glu, w, cb, g, b)


def _conv_sample_kernel(ext_ref, w_ref, cb_ref, g_ref, b_ref, yb_ref):
    t_new = yb_ref.shape[0]
    for t in range(t_new):
        acc = jnp.zeros(ext_ref.shape[1:], F32)
        for k in range(CONV_W):
            acc = acc + ext_ref[t + k] * w_ref[k:k + 1, :]
        y = _layernorm(acc + cb_ref[...], g_ref[...], b_ref[...])
        yb_ref[t] = (y * _sigmoid(y)).astype(BF16)


def _conv_sample(ext_t, w, cb, g, b, layer):
    te, s, d = ext_t.shape
    t_new = te - (CONV_W - 1)
    sb = 32 if s % 32 == 0 else s
    vec = pl.BlockSpec((None, 1, d), lambda i: (layer, 0, 0))
    return pl.pallas_call(
        _conv_sample_kernel,
        grid=(s // sb,),
        in_specs=[pl.BlockSpec((te, sb, d), lambda i: (0, i, 0)),
                  pl.BlockSpec((None, CONV_HALO, d), lambda i: (layer, 0, 0)),
                  vec, vec, vec],
        out_specs=pl.BlockSpec((t_new, sb, d), lambda i: (0, i, 0)),
        out_shape=jax.ShapeDtypeStruct((t_new, s, d), BF16),
        compiler_params=_params("arbitrary"),
        name="conv_sample",
    )(ext_t, w, cb, g, b)


def _attend(q, k, v, hd):
    outs = []
    for h in range(q.shape[1] // hd):
        sl = slice(h * hd, (h + 1) * hd)
        s = _dot_nt(q[:, sl], k[:, sl].astype(BF16)) * (hd ** -0.5)
        s = s - jnp.max(s, axis=-1, keepdims=True)
        p = jnp.exp(s)
        p = p / jnp.sum(p, axis=-1, keepdims=True)
        outs.append(_dot(p.astype(BF16), v[:, sl].astype(BF16)))
    return outs


def _attn_prompt_kernel(q_ref, k_ref, v_ref, ym_ref, *, hd):
    outs = _attend(q_ref[...], k_ref[...], v_ref[...], hd)
    for h, o in enumerate(outs):
        ym_ref[:, h * hd:(h + 1) * hd] = o.astype(BF16)


def _attn_prompt(qm, k, v, layer, batch, seq):
    dm = qm.shape[1]
    n_mem = k.shape[1] // batch
    tq = _tile(seq, 512)
    nt = seq // tq
    kv = pl.BlockSpec((None, n_mem, dm), lambda n, t: (layer, n, 0))
    return pl.pallas_call(
        functools.partial(_attn_prompt_kernel, hd=dm // M_HEADS),
        grid=(batch, nt),
        in_specs=[pl.BlockSpec((tq, dm), lambda n, t: (n * nt + t, 0)), kv, kv],
        out_specs=pl.BlockSpec((tq, dm), lambda n, t: (n * nt + t, 0)),
        out_shape=jax.ShapeDtypeStruct((batch * seq, dm), BF16),
        compiler_params=_params("arbitrary", "arbitrary"),
        name="attn_prompt",
    )(qm, k, v)


def _attn_sample_kernel(q_ref, k_ref, v_ref, ym_ref, *, hd, dec_seq, n_mem):
    rows = q_ref.shape[0]
    n_half = hd // LANES
    stride = n_half * M_HEADS
    q = q_ref[...]
    seq_of_row = lax.shift_right_logical(lax.broadcasted_iota(jnp.int32, (rows, LANES), 0), int(math.log2(dec_seq)))
    for h in range(M_HEADS):
        acc = [jnp.zeros((rows, LANES), F32) for _ in range(n_half)]
        for s in range(rows // dec_seq):
            sc = None
            for c in range(n_half):
                kc = k_ref[s, pl.ds(c * M_HEADS + h, n_mem, stride=stride), :].astype(BF16)
                part = _dot_nt(q[:, h * hd + c * LANES:h * hd + (c + 1) * LANES], kc)
                sc = part if sc is None else sc + part
            sc = sc * (hd ** -0.5)
            sc = sc - jnp.max(sc, axis=-1, keepdims=True)
            p = jnp.exp(sc)
            p = (p / jnp.sum(p, axis=-1, keepdims=True)).astype(BF16)
            for c in range(n_half):
                vc = v_ref[s, pl.ds(c * M_HEADS + h, n_mem, stride=stride), :].astype(BF16)
                acc[c] = jnp.where(seq_of_row == s, _dot(p, vc), acc[c])
        for c in range(n_half):
            ym_ref[:, h * hd + c * LANES:h * hd + (c + 1) * LANES] = acc[c].astype(BF16)


def _attn_sample(qm, k, v, layer, row0, dec_batch, dec_seq):
    dm = qm.shape[1]
    n_mem = k.shape[2] * LANES // dm
    rows = 16
    spb = rows // dec_seq
    assert rows % dec_seq == 0 and dec_batch % spb == 0 and row0 % rows == 0
    kv = pl.BlockSpec((None, spb, k.shape[2], LANES), lambda i: (layer, i, 0, 0))
    return pl.pallas_call(
        functools.partial(_attn_sample_kernel, hd=dm // M_HEADS, dec_seq=dec_seq, n_mem=n_mem),
        grid=(dec_batch // spb,),
        in_specs=[pl.BlockSpec((rows, dm), lambda i: (row0 // rows + i, 0)), kv, kv],
        out_specs=pl.BlockSpec((rows, dm), lambda i: (i, 0)),
        out_shape=jax.ShapeDtypeStruct((dec_batch * dec_seq, dm), BF16),
        compiler_params=_params("arbitrary"),
        name="attn_sample",
    )(qm, k, v)


def _merge_kernel(x_ref, ya_ref, ybp_ref, ybs_ref, ymp_ref, yms_ref, ga_ref, gb_ref, gm_ref,
                  woa_ref, wob_ref, wom_ref, wout_ref, g_ref, xo_ref, xnt_ref, *, n_prompt_tiles):
    is_prompt = pl.program_id(0) < n_prompt_tiles
    yb = jnp.where(is_prompt, ybp_ref[...], ybs_ref[...])
    ym = jnp.where(is_prompt, ymp_ref[...], yms_ref[...])
    merged = (ga_ref[...] * _dot(ya_ref[...], woa_ref[...])
              + gb_ref[...] * _dot(yb, wob_ref[...])
              + gm_ref[...] * _dot(ym, wom_ref[...]))
    x = x_ref[...] + _dot(merged.astype(BF16), wout_ref[...])
    xo_ref[...] = x
    xnt_ref[...] = _rms(x, g_ref[...]).T.astype(BF16)


def _merge(x, ya, yb_p, yb_s, ym_p, ym_s, ga, gb, gm, woa, wob, wom, wout, g, layer):
    n, d = x.shape
    n_p, n_s = yb_p.shape[0], yb_s.shape[0]
    tm = _tile(math.gcd(n_p, n_s), 256)
    npt, nst = n_p // tm, n_s // tm
    tok = pl.BlockSpec((tm, d), lambda i: (i, 0))
    tok_p = pl.BlockSpec((tm, d), lambda i: (jnp.minimum(i, npt - 1), 0))
    tok_s = pl.BlockSpec((tm, d), lambda i: (jnp.clip(i - npt, 0, nst - 1), 0))
    wgt = pl.BlockSpec((None, d, d), lambda i: (layer, 0, 0))
    return pl.pallas_call(
        functools.partial(_merge_kernel, n_prompt_tiles=npt),
        grid=(n // tm,),
        in_specs=[tok, tok, tok_p, tok_s, tok_p, tok_s, tok, tok, tok] + [wgt] * 4
                 + [pl.BlockSpec((None, 1, d), lambda i: (layer, 0, 0))],
        out_specs=[tok, pl.BlockSpec((d, tm), lambda i: (0, i))],
        out_shape=[jax.ShapeDtypeStruct((n, d), F32), jax.ShapeDtypeStruct((d, n), BF16)],
        compiler_params=_params("arbitrary"),
        name="merge",
    )(x, ya, yb_p, yb_s, ym_p, ym_s, ga, gb, gm, woa, wob, wom, wout, g)


def _extract_topk(s, vals_sc, exact):
    cur = s
    if not exact:
        for r in range(PEER_TOPK):
            m = jnp.max(cur, axis=0, keepdims=True)
            cur = jnp.where(cur == m, -(_MARK_BASE + r * _MARK_STEP), cur)
            vals_sc[r:r + 1, :] = m
        return jnp.where(cur <= -_MARK_BASE, (-cur - _MARK_BASE) * (1.0 / _MARK_STEP), float(PEER_TOPK))
    idx = lax.broadcasted_iota(jnp.int32, s.shape, 0).astype(F32)
    rank = jnp.full(s.shape, float(PEER_TOPK), F32)
    for r in range(PEER_TOPK):
        m = jnp.max(cur, axis=0, keepdims=True)
        first = jnp.min(jnp.where(cur == m, idx, float(s.shape[0])), axis=0, keepdims=True)
        sel = idx == first
        rank = jnp.where(sel, float(r), rank)
        cur = jnp.where(sel, NEG_INF, cur)
        vals_sc[r:r + 1, :] = m
    return rank


def _take_candidates(cand, cand_idx, exact):
    cur = cand
    for _ in range(PEER_TOPK):
        m = jnp.max(cur, axis=0, keepdims=True)
        if exact:
            first = jnp.min(jnp.where(cur == m, cand_idx, float(_CAND_ROWS)), axis=0, keepdims=True)
            cur = jnp.where(cand_idx == first, NEG_INF, cur)
        else:
            cur = jnp.where(cur == m, NEG_INF, cur)
    return jnp.where((cur == NEG_INF) & (cand_idx < float(len(_CAND))), 1.0, 0.0)


def _peer_route_kernel(xnt_ref, wqt_ref, keys_ref, r2_ref, a2_ref, bi_ref, c1_ref,
                       v1_sc, v2_sc, cand_sc, rank_sc, taken_sc, count_sc):
    qt = _dot(wqt_ref[...], xnt_ref[...])
    tl = qt.shape[1]
    cand_idx = lax.broadcasted_iota(jnp.int32, (_CAND_ROWS, tl), 0).astype(F32)
    cand_sc[...] = jnp.full((_CAND_ROWS, tl), NEG_INF, F32)

    def miscount(flags):
        return jnp.abs(jnp.sum(flags, axis=0, keepdims=True) - float(PEER_TOPK))

    for h in range(PEER_HEADS):
        s1 = _dot(keys_ref[2 * h], qt[(2 * h) * PEER_HALF:(2 * h + 1) * PEER_HALF, :].astype(BF16))
        s2 = _dot(keys_ref[2 * h + 1], qt[(2 * h + 1) * PEER_HALF:(2 * h + 2) * PEER_HALF, :].astype(BF16))

        def route(exact):
            rank_sc[0] = _extract_topk(s1, v1_sc, exact)
            rank_sc[1] = _extract_topk(s2, v2_sc, exact)
            for p, (a, b) in enumerate(_CAND):
                cand_sc[p:p + 1, :] = v1_sc[a:a + 1, :] + v2_sc[b:b + 1, :]
            taken = _take_candidates(cand_sc[...], cand_idx, exact)
            taken_sc[...] = taken
            if exact:
                row0 = 0
                for a, n_a in enumerate(_CAND_WIDTH):
                    in_a = (cand_idx >= float(row0)) & (cand_idx < float(row0 + n_a))
                    count_sc[a:a + 1, :] = jnp.sum(jnp.where(in_a, taken, 0.0), axis=0, keepdims=True)
                    row0 += n_a
            else:
                tau = jnp.min(jnp.where(taken > 0.0, cand_sc[...], float("inf")), axis=0, keepdims=True)
                v2 = v2_sc[...]
                b_idx = lax.broadcasted_iota(jnp.int32, v2.shape, 0)
                for a, n_a in enumerate(_CAND_WIDTH):
                    hit = (v1_sc[a:a + 1, :] + v2 >= tau) & (b_idx < n_a)
                    count_sc[a:a + 1, :] = jnp.sum(jnp.where(hit, 1.0, 0.0), axis=0, keepdims=True)

        route(False)
        ranked = lambda r: jnp.where(r < float(PEER_TOPK), 1.0, 0.0)
        tied = jnp.max(miscount(ranked(rank_sc[0])) + miscount(ranked(rank_sc[1])) + miscount(taken_sc[...]))

        @pl.when(tied > 0.0)
        def _():
            route(True)

        rank1, rank2, taken, cand = rank_sc[0], rank_sc[1], taken_sc[...], cand_sc[...]
        top = v1_sc[0:1, :] + v2_sc[0:1, :]
        z = jnp.sum(taken * jnp.exp(cand - top), axis=0, keepdims=True)
        bi = jnp.zeros_like(s1)
        for a in range(PEER_TOPK):
            bi = jnp.where(rank1 == float(a), count_sc[a:a + 1, :], bi)
        r2_ref[h] = rank2.astype(BF16)
        a2_ref[h] = jnp.exp(s2 - v2_sc[0:1, :]).astype(BF16)
        bi_ref[h] = bi
        c1_ref[h] = jnp.exp(s1 - v1_sc[0:1, :]) * (0.5 / z)


def _peer_route(xnt, wqt, keys, layer):
    d, n = xnt.shape
    tl = _tile(n, 256)
    rows = pl.BlockSpec((PEER_HEADS, N_KEYS, tl), lambda i: (0, 0, i))
    return pl.pallas_call(
        _peer_route_kernel,
        grid=(n // tl,),
        in_specs=[pl.BlockSpec((d, tl), lambda i: (0, i)),
                  pl.BlockSpec((None, d, d), lambda i: (layer, 0, 0)),
                  pl.BlockSpec((None, 2 * PEER_HEADS, N_KEYS, PEER_HALF), lambda i: (layer, 0, 0, 0))],
        out_specs=[rows] * 4,
        out_shape=[jax.ShapeDtypeStruct((PEER_HEADS, N_KEYS, n), dt) for dt in (BF16, BF16, F32, F32)],
        scratch_shapes=[pltpu.VMEM((PEER_TOPK, tl), F32), pltpu.VMEM((PEER_TOPK, tl), F32),
                        pltpu.VMEM((_CAND_ROWS, tl), F32), pltpu.VMEM((2, N_KEYS, tl), F32),
                        pltpu.VMEM((_CAND_ROWS, tl), F32), pltpu.VMEM((PEER_TOPK, tl), F32)],
        compiler_params=_params("arbitrary"),
        name="peer_route",
    )(xnt, wqt, keys)


def _peer_dense_kernel(xnt_ref, u_ref, vt_ref, r2_ref, a2_ref, bi_ref, c1_ref, x_ref, g_ref, xo_ref, acc_sc, p_sc,
                       *, final_norm, n_chunks):
    e = pl.program_id(1)
    n_eb = pl.num_programs(1) - 1
    eb, tm = u_ref.shape[0], xnt_ref.shape[1]
    ce = eb // n_chunks
    cur = lax.rem(e, 2)
    prev = 1 - cur
    zero = jnp.zeros((), BF16)

    def row_tile(ref, h, e1):
        row = jnp.broadcast_to(ref[h, pl.ds(e1, 1), :], (BF16_ROWS, tm)).astype(BF16)
        return jnp.concatenate([row] * (N_KEYS // BF16_ROWS), axis=0)

    def build(drain):
        xnt = xnt_ref[...]
        hids = [_dot(u_ref[c * ce:(c + 1) * ce, :], xnt) for c in range(n_chunks)]
        if drain:
            acc_sc[...] += _dot(vt_ref[...], p_sc[prev])
        for c in range(n_chunks):
            act = (hids[c] * (1.0 + lax.erf(hids[c] * (2.0 ** -0.5)))).astype(BF16)
            for sub in range(ce // N_KEYS):
                r0 = c * ce + sub * N_KEYS
                e1 = e * (eb // N_KEYS) + r0 // N_KEYS
                w = None
                for h in range(PEER_HEADS):
                    term = jnp.where(r2_ref[h] < row_tile(bi_ref, h, e1), a2_ref[h], zero) * row_tile(c1_ref, h, e1)
                    w = term if w is None else w + term
                p_sc[cur, r0:r0 + N_KEYS, :] = w * act[sub * N_KEYS:(sub + 1) * N_KEYS, :]

    @pl.when(e == 0)
    def _():
        acc_sc[...] = jnp.zeros_like(acc_sc)
        build(drain=False)

    @pl.when((e > 0) & (e < n_eb))
    def _():
        build(drain=True)

    @pl.when(e == n_eb)
    def _():
        x = x_ref[...] + (acc_sc[...] + _dot(vt_ref[...], p_sc[prev])).T
        xo_ref[...] = _rms(x, g_ref[...]) if final_norm else x


def _peer_dense(xnt, u, vt, r2, a2, bi, c1, x, g, layer, final_norm):
    d, n = xnt.shape
    n_exp = u.shape[1]
    tm = _tile(n, 512, 2 * LANES)
    eb, n_chunks = 2048, 4
    n_eb = n_exp // eb
    assert n_exp % eb == 0 and n_exp == N_KEYS * N_KEYS
    tab = pl.BlockSpec((PEER_HEADS, N_KEYS, tm), lambda i, e: (0, 0, i))
    tok = pl.BlockSpec((tm, d), lambda i, e: (i, 0))
    return pl.pallas_call(
        functools.partial(_peer_dense_kernel, final_norm=final_norm, n_chunks=n_chunks),
        grid=(n // tm, n_eb + 1),
        in_specs=[pl.BlockSpec((d, tm), lambda i, e: (0, i)),
                  pl.BlockSpec((None, eb, d), lambda i, e: (layer, jnp.minimum(e, n_eb - 1), 0)),
                  pl.BlockSpec((None, d, eb), lambda i, e: (layer, 0, jnp.maximum(e - 1, 0))),
                  tab, tab, tab, tab, tok,
                  pl.BlockSpec((1, d), lambda i, e: (0, 0))],
        out_specs=tok,
        out_shape=jax.ShapeDtypeStruct((n, d), F32),
        scratch_shapes=[pltpu.VMEM((d, tm), F32), pltpu.VMEM((2, eb, tm), BF16)],
        compiler_params=_params("arbitrary", "arbitrary"),
        name="peer_dense",
    )(xnt, u, vt, r2, a2, bi, c1, x, g)


def kernel(x_prompt, x_sample, mem_prompt, cache_mem_k, cache_mem_v, state_conv, norm_mix_g, w_in, a_ln_g, a_ln_b, a_ws, a_bs, b_conv_w, b_conv_b, b_ln_g, b_ln_b, mem_norm_g, w_mem_k, w_mem_v, w_o_a, w_o_b, w_o_m, w_out, norm_ffn_g, peer_wq, peer_keys, peer_u, peer_v, final_norm_g):
    batch, seq, d = x_prompt.shape
    dec_batch, dec_seq, _ = x_sample.shape
    depth = w_in.shape[0]
    n_mem = mem_prompt.shape[1]
    n_p, n_s = batch * seq, dec_batch * dec_seq
    hd = d // M_HEADS

    vec = lambda a: a.reshape(depth, 1, -1)
    w_in_b = w_in.astype(BF16)
    wk_b, wv_b = w_mem_k.astype(BF16), w_mem_v.astype(BF16)
    woa_b, wob_b, wom_b, wout_b = (w.astype(BF16) for w in (w_o_a, w_o_b, w_o_m, w_out))
    wqt_b = jnp.swapaxes(peer_wq, 1, 2).astype(BF16)
    keys_b = peer_keys.reshape(depth, 2 * PEER_HEADS, N_KEYS, PEER_HALF).astype(BF16)
    u_b = peer_u.astype(BF16)
    vt_b = jnp.swapaxes(peer_v, 1, 2).astype(BF16)
    conv_w = jnp.pad(b_conv_w, ((0, 0), (0, CONV_HALO - CONV_W), (0, 0)))
    reps = CHUNK // dec_seq
    ws_eff = jnp.stack([a_ws, jnp.tile(a_ws[:, :, :dec_seq, :dec_seq], (1, 1, reps, reps))], axis=1)
    bs_rows = jnp.repeat(jnp.swapaxes(a_bs, 1, 2), d // A_GROUPS, axis=2)
    bs_eff = jnp.stack([bs_rows, jnp.tile(bs_rows[:, :dec_seq], (1, reps, 1))], axis=1)

    mk_p, mv_p = _memkv(mem_prompt.reshape(batch * n_mem, d), vec(mem_norm_g), wk_b, wv_b)
    def lane_rows(c):
        c = c.reshape(depth, dec_batch, n_mem, M_HEADS, hd // LANES, LANES)
        return jnp.swapaxes(c, 3, 4).reshape(depth, dec_batch, n_mem * (hd // LANES) * M_HEADS, LANES)

    cache_k, cache_v = lane_rows(cache_mem_k), lane_rows(cache_mem_v)

    x = jnp.concatenate([x_prompt.reshape(n_p, d), x_sample.reshape(n_s, d)], axis=0)
    conv_p, conv_s, chunk_v_s = [], [], []
    for l in range(depth):
        ua, vn, glu, qm, ga, gb, gm = _inproj(x, vec(norm_mix_g), w_in_b, vec(a_ln_g), vec(a_ln_b), l)
        ya = _sgu(ua, vn, ws_eff[l], bs_eff[l], n_p, dec_seq)
        ext_s = jnp.concatenate([state_conv[l], glu[n_p:].reshape(dec_batch, dec_seq, d)], axis=1)
        yb_p = _conv_prompt(glu, conv_w, vec(b_conv_b), vec(b_ln_g), vec(b_ln_b), l, batch, seq)
        yb_s = _conv_sample(jnp.swapaxes(ext_s, 0, 1), conv_w, vec(b_conv_b), vec(b_ln_g), vec(b_ln_b), l)
        yb_s = jnp.swapaxes(yb_s, 0, 1).reshape(n_s, d)
        ym_p = _attn_prompt(qm, mk_p, mv_p, l, batch, seq)
        ym_s = _attn_sample(qm, cache_k, cache_v, l, n_p, dec_batch, dec_seq)
        x, xnt = _merge(x, ya, yb_p, yb_s, ym_p, ym_s, ga, gb, gm, woa_b, wob_b, wom_b, wout_b, vec(norm_ffn_g), l)
        r2, a2, bi, c1 = _peer_route(xnt, wqt_b, keys_b, l)
        x = _peer_dense(xnt, u_b, vt_b, r2, a2, bi, c1, x, final_norm_g.reshape(1, d), l, l == depth - 1)
        conv_p.append(jnp.stack([glu[(b + 1) * seq - (CONV_W - 1):(b + 1) * seq] for b in range(batch)]))
        conv_s.append(ext_s[:, dec_seq:])
        chunk_v_s.append(vn[n_p:].reshape(dec_batch, dec_seq, d))

    y_prompt = x[:n_p].reshape(batch, seq, d)
    y_sample = x[n_p:].reshape(dec_batch, dec_seq, d)
    new_mem_k = mk_p.reshape(depth, batch, n_mem, M_HEADS, hd)
    new_mem_v = mv_p.reshape(depth, batch, n_mem, M_HEADS, hd)
    return (y_prompt, y_sample, new_mem_k, new_mem_v,
            jnp.stack(conv_p), jnp.stack(conv_s), jnp.stack(chunk_v_s))
```

```python
import functools
import math

import jax
import jax.numpy as jnp
from jax import lax
from jax.experimental import pallas as pl
from jax.experimental.pallas import tpu as pltpu

F32 = jnp.float32
BF16 = jnp.bfloat16
EPS = 1e-6
LANES = 128
SUBLANES = 8
BF16_ROWS = 16
VMEM_LIMIT_BYTES = 56 * 1024 * 1024

CHUNK = 128
A_GROUPS = 8
CONV_W = 31
CONV_HALO = 32
M_HEADS = 4
PEER_HEADS = 8
N_KEYS = 128
PEER_HALF = 64
PEER_TOPK = 16
NEG_INF = float("-inf")
_MARK_BASE = 2.0 ** 127
_MARK_STEP = 2.0 ** 122

_CAND = [(a, b) for a in range(PEER_TOPK) for b in range(PEER_TOPK) if (a + 1) * (b + 1) <= PEER_TOPK]
_CAND_ROWS = -(-len(_CAND) // 8) * 8
_CAND_WIDTH = [sum(1 for (a, _) in _CAND if a == r) for r in range(PEER_TOPK)]


def _tile(n, target, quantum=LANES):
    best = None
    t = quantum
    while t <= min(n, target):
        if n % t == 0:
            best = t
        t += quantum
    assert best is not None, (n, target, quantum)
    return best


def _params(*sem):
    return pltpu.CompilerParams(dimension_semantics=sem, vmem_limit_bytes=VMEM_LIMIT_BYTES)


def _gelu(x):
    return 0.5 * x * (1.0 + lax.erf(x * (2.0 ** -0.5)))


def _sigmoid(x):
    return 1.0 / (1.0 + jnp.exp(-x))


def _rms(x, g):
    ms = jnp.mean(x * x, axis=-1, keepdims=True)
    return x * lax.rsqrt(ms + EPS) * g


def _layernorm(x, g, b):
    mu = jnp.mean(x, axis=-1, keepdims=True)
    xc = x - mu
    var = jnp.mean(xc * xc, axis=-1, keepdims=True)
    return xc * lax.rsqrt(var + EPS) * g + b


def _dot(a, b):
    return jnp.dot(a, b, preferred_element_type=F32)


def _dot_nt(a, b):
    return lax.dot_general(a, b, (((1,), (1,)), ((), ())), preferred_element_type=F32)


def _memkv_kernel(mem_ref, g_ref, wk_ref, wv_ref, k_ref, v_ref):
    mn = _rms(mem_ref[...], g_ref[...]).astype(BF16)
    k_ref[...] = _dot(mn, wk_ref[...])
    v_ref[...] = _dot(mn, wv_ref[...])


def _memkv(mem, g, wk, wv):
    r, d = mem.shape
    depth, _, dm = wk.shape
    tm = _tile(r, 512)
    return pl.pallas_call(
        _memkv_kernel,
        grid=(depth, r // tm),
        in_specs=[
            pl.BlockSpec((tm, d), lambda l, i: (i, 0)),
            pl.BlockSpec((None, 1, d), lambda l, i: (l, 0, 0)),
            pl.BlockSpec((None, d, dm), lambda l, i: (l, 0, 0)),
            pl.BlockSpec((None, d, dm), lambda l, i: (l, 0, 0)),
        ],
        out_specs=[
            pl.BlockSpec((None, tm, dm), lambda l, i: (l, i, 0)),
            pl.BlockSpec((None, tm, dm), lambda l, i: (l, i, 0)),
        ],
        out_shape=[jax.ShapeDtypeStruct((depth, r, dm), F32)] * 2,
        compiler_params=_params("arbitrary", "arbitrary"),
        name="memkv",
    )(mem, g, wk, wv)


def _inproj_kernel(x_ref, g_ref, w_ref, lng_ref, lnb_ref, ws_ref, bs_ref,
                   ya_ref, vns_ref, glu_ref, qm_ref, ga_ref, gb_ref, gm_ref, xn_sc, ua_sc, val_sc,
                   *, n_prompt_tiles, sample_shift):
    i, j = pl.program_id(0), pl.program_id(1)
    n_chunks = 4
    cw = w_ref.shape[1] // n_chunks

    @pl.when(j == 0)
    def _():
        xn_sc[...] = _rms(x_ref[...], g_ref[...]).astype(BF16)

    def project():
        xn = xn_sc[...]
        return [_dot(xn, w_ref[:, c * cw:(c + 1) * cw]) for c in range(n_chunks)]

    def segment(seg, out_ref, fn):
        @pl.when(j == seg)
        def _():
            for c, h in enumerate(project()):
                cs = slice(c * cw, (c + 1) * cw)
                out_ref[:, cs] = fn(h, cs).astype(out_ref.dtype)

    segment(0, ua_sc, lambda h, cs: _gelu(h))

    @pl.when(j == 1)
    def _():
        v = jnp.concatenate([_gelu(h) for h in project()], axis=1)
        vn = _layernorm(v, lng_ref[...], lnb_ref[...])
        vns_ref[...] = vn
        row = lax.broadcasted_iota(jnp.int32, (CHUNK, CHUNK), 0)
        col = lax.broadcasted_iota(jnp.int32, (CHUNK, CHUNK), 1)
        same_seq = lax.shift_right_logical(row, sample_shift) == lax.shift_right_logical(col, sample_shift)
        keep = (col <= row) & (same_seq | (i < n_prompt_tiles))
        vb = vn.astype(BF16)
        for grp in range(A_GROUPS):
            sl = slice(grp * LANES, (grp + 1) * LANES)
            w = jnp.where(keep, ws_ref[grp], 0.0).astype(BF16)
            for r0 in range(0, vn.shape[0], CHUNK):
                s = _dot(w, vb[r0:r0 + CHUNK, sl]) + bs_ref[:, sl]
                ya_ref[r0:r0 + CHUNK, sl] = (ua_sc[r0:r0 + CHUNK, sl].astype(F32) * s).astype(BF16)

    segment(2, val_sc, lambda h, cs: h)
    segment(3, glu_ref, lambda h, cs: val_sc[:, cs] * _sigmoid(h))
    segment(4, qm_ref, lambda h, cs: h)
    segment(5, ga_ref, lambda h, cs: _sigmoid(h))
    segment(6, gb_ref, lambda h, cs: _sigmoid(h))
    segment(7, gm_ref, lambda h, cs: _sigmoid(h))


def _inproj(x, g, w_in, lng, lnb, ws_eff, bs_eff, layer, n_prompt, dec_seq):
    n, d = x.shape
    n_s = n - n_prompt
    tm = _tile(math.gcd(n_prompt, n_s), 512)
    npt = n_prompt // tm
    nseg = w_in.shape[2] // d
    shift = int(math.log2(dec_seq))
    assert nseg == 8 and 1 << shift == dec_seq and CHUNK % dec_seq == 0
    tok = pl.BlockSpec((tm, d), lambda i, j: (i, 0))
    vec = pl.BlockSpec((None, 1, d), lambda i, j: (layer, 0, 0))
    sel = lambda i: jnp.where(i >= npt, 1, 0)
    return pl.pallas_call(
        functools.partial(_inproj_kernel, n_prompt_tiles=npt, sample_shift=shift),
        grid=(n // tm, nseg),
        in_specs=[tok, vec, pl.BlockSpec((None, d, d), lambda i, j: (layer, 0, j)), vec, vec,
                  pl.BlockSpec((None, None, A_GROUPS, CHUNK, CHUNK), lambda i, j: (layer, sel(i), 0, 0, 0)),
                  pl.BlockSpec((None, None, CHUNK, d), lambda i, j: (layer, sel(i), 0, 0))],
        out_specs=[tok, pl.BlockSpec((tm, d), lambda i, j: (jnp.maximum(i - npt, 0), 0))] + [tok] * 5,
        out_shape=[jax.ShapeDtypeStruct((n, d), BF16), jax.ShapeDtypeStruct((n_s, d), F32)]
                  + [jax.ShapeDtypeStruct((n, d), dt) for dt in (F32, BF16, BF16, BF16, BF16)],
        scratch_shapes=[pltpu.VMEM((tm, d), BF16), pltpu.VMEM((tm, d), BF16), pltpu.VMEM((tm, d), F32)],
        compiler_params=_params("arbitrary", "arbitrary"),
        name="inproj",
    )(x, g, w_in, lng, lnb, ws_eff, bs_eff)


def _conv_prompt_kernel(cur_ref, halo_ref, w_ref, cb_ref, g_ref, b_ref, yb_ref, ext_sc, c_sc):
    t = pl.program_id(1)
    tc, d = cur_ref.shape
    halo = halo_ref[...]
    ext_sc[0:CONV_HALO, :] = jnp.where(t == 0, 0.0, halo)
    ext_sc[CONV_HALO:, :] = cur_ref[...]
    off = CONV_HALO - (CONV_W - 1)
    for cb in range(d // LANES):
        sl = slice(cb * LANES, (cb + 1) * LANES)
        acc = jnp.zeros((tc, LANES), F32)
        for k in range(CONV_W):
            acc = acc + ext_sc[off + k:off + k + tc, sl] * w_ref[k:k + 1, sl]
        c_sc[:, sl] = acc
    y = _layernorm(c_sc[...] + cb_ref[...], g_ref[...], b_ref[...])
    yb_ref[...] = (y * _sigmoid(y)).astype(BF16)


def _conv_prompt(glu, w, cb, g, b, layer, batch, seq):
    d = glu.shape[1]
    tc = _tile(seq, 256)
    nt = seq // tc
    hpt = tc // CONV_HALO
    vec = pl.BlockSpec((None, 1, d), lambda n, t: (layer, 0, 0))
    return pl.pallas_call(
        _conv_prompt_kernel,
        grid=(batch, nt),
        in_specs=[
            pl.BlockSpec((tc, d), lambda n, t: (n * nt + t, 0)),
            pl.BlockSpec((CONV_HALO, d), lambda n, t: (jnp.maximum((n * nt + t) * hpt - 1, 0), 0)),
            pl.BlockSpec((None, CONV_HALO, d), lambda n, t: (layer, 0, 0)),
            vec, vec, vec,
        ],
        out_specs=pl.BlockSpec((tc, d), lambda n, t: (n * nt + t, 0)),
        out_shape=jax.ShapeDtypeStruct((batch * seq, d), BF16),
        scratch_shapes=[pltpu.VMEM((CONV_HALO + tc, d), F32), pltpu.VMEM((tc, d), F32)],
        compiler_params=_params("arbitrary", "arbitrary"),
        name="conv_prompt",
    )(glu, glu, w, cb, g, b)


def _conv_sample_kernel(ext_ref, w_ref, cb_ref, g_ref, b_ref, yb_ref):
    t_new = yb_ref.shape[0]
    for t in range(t_new):
        acc = jnp.zeros(ext_ref.shape[1:], F32)
        for k in range(CONV_W):
            acc = acc + ext_ref[t + k] * w_ref[k:k + 1, :]
        y = _layernorm(acc + cb_ref[...], g_ref[...], b_ref[...])
        yb_ref[t] = (y * _sigmoid(y)).astype(BF16)


def _conv_sample(ext_t, w, cb, g, b, layer):
    te, s, d = ext_t.shape
    t_new = te - (CONV_W - 1)
    sb = 32 if s % 32 == 0 else s
    vec = pl.BlockSpec((None, 1, d), lambda i: (layer, 0, 0))
    return pl.pallas_call(
        _conv_sample_kernel,
        grid=(s // sb,),
        in_specs=[pl.BlockSpec((te, sb, d), lambda i: (0, i, 0)),
                  pl.BlockSpec((None, CONV_HALO, d), lambda i: (layer, 0, 0)),
                  vec, vec, vec],
        out_specs=pl.BlockSpec((t_new, sb, d), lambda i: (0, i, 0)),
        out_shape=jax.ShapeDtypeStruct((t_new, s, d), BF16),
        compiler_params=_params("arbitrary"),
        name="conv_sample",
    )(ext_t, w, cb, g, b)


def _attend(q, k, v, hd):
    outs = []
    for h in range(q.shape[1] // hd):
        sl = slice(h * hd, (h + 1) * hd)
        s = _dot_nt(q[:, sl], k[:, sl].astype(BF16)) * (hd ** -0.5)
        s = s - jnp.max(s, axis=-1, keepdims=True)
        p = jnp.exp(s)
        p = p / jnp.sum(p, axis=-1, keepdims=True)
        outs.append(_dot(p.astype(BF16), v[:, sl].astype(BF16)))
    return outs


def _attn_prompt_kernel(q_ref, k_ref, v_ref, ym_ref, *, hd):
    outs = _attend(q_ref[...], k_ref[...], v_ref[...], hd)
    for h, o in enumerate(outs):
        ym_ref[:, h * hd:(h + 1) * hd] = o.astype(BF16)


def _attn_prompt(qm, k, v, layer, batch, seq):
    dm = qm.shape[1]
    n_mem = k.shape[1] // batch
    tq = _tile(seq, 512)
    nt = seq // tq
    kv = pl.BlockSpec((None, n_mem, dm), lambda n, t: (layer, n, 0))
    return pl.pallas_call(
        functools.partial(_attn_prompt_kernel, hd=dm // M_HEADS),
        grid=(batch, nt),
        in_specs=[pl.BlockSpec((tq, dm), lambda n, t: (n * nt + t, 0)), kv, kv],
        out_specs=pl.BlockSpec((tq, dm), lambda n, t: (n * nt + t, 0)),
        out_shape=jax.ShapeDtypeStruct((batch * seq, dm), BF16),
        compiler_params=_params("arbitrary", "arbitrary"),
        name="attn_prompt",
    )(qm, k, v)


def _attn_sample_kernel(q_ref, k_ref, v_ref, ym_ref, *, hd, dec_seq, n_mem):
    rows = q_ref.shape[0]
    n_half = hd // LANES
    stride = n_half * M_HEADS
    q = q_ref[...]
    seq_of_row = lax.shift_right_logical(lax.broadcasted_iota(jnp.int32, (rows, LANES), 0), int(math.log2(dec_seq)))
    for h in range(M_HEADS):
        acc = [jnp.zeros((rows, LANES), F32) for _ in range(n_half)]
        for s in range(rows // dec_seq):
            sc = None
            for c in range(n_half):
                kc = k_ref[s, pl.ds(c * M_HEADS + h, n_mem, stride=stride), :].astype(BF16)
                part = _dot_nt(q[:, h * hd + c * LANES:h * hd + (c + 1) * LANES], kc)
                sc = part if sc is None else sc + part
            sc = sc * (hd ** -0.5)
            sc = sc - jnp.max(sc, axis=-1, keepdims=True)
            p = jnp.exp(sc)
            p = (p / jnp.sum(p, axis=-1, keepdims=True)).astype(BF16)
            for c in range(n_half):
                vc = v_ref[s, pl.ds(c * M_HEADS + h, n_mem, stride=stride), :].astype(BF16)
                acc[c] = jnp.where(seq_of_row == s, _dot(p, vc), acc[c])
        for c in range(n_half):
            ym_ref[:, h * hd + c * LANES:h * hd + (c + 1) * LANES] = acc[c].astype(BF16)


def _attn_sample(qm, k, v, layer, row0, dec_batch, dec_seq):
    dm = qm.shape[1]
    n_mem = k.shape[2] * LANES // dm
    rows = 16
    spb = rows // dec_seq
    assert rows % dec_seq == 0 and dec_batch % spb == 0 and row0 % rows == 0
    kv = pl.BlockSpec((None, spb, k.shape[2], LANES), lambda i: (layer, i, 0, 0))
    return pl.pallas_call(
        functools.partial(_attn_sample_kernel, hd=dm // M_HEADS, dec_seq=dec_seq, n_mem=n_mem),
        grid=(dec_batch // spb,),
        in_specs=[pl.BlockSpec((rows, dm), lambda i: (row0 // rows + i, 0)), kv, kv],
        out_specs=pl.BlockSpec((rows, dm), lambda i: (i, 0)),
        out_shape=jax.ShapeDtypeStruct((dec_batch * dec_seq, dm), BF16),
        compiler_params=_params("arbitrary"),
        name="attn_sample",
    )(qm, k, v)


def _merge_kernel(x_ref, ya_ref, ybp_ref, ybs_ref, ymp_ref, yms_ref, ga_ref, gb_ref, gm_ref,
                  woa_ref, wob_ref, wom_ref, wout_ref, g_ref, xo_ref, xnt_ref, *, n_prompt_tiles):
    is_prompt = pl.program_id(0) < n_prompt_tiles
    yb = jnp.where(is_prompt, ybp_ref[...], ybs_ref[...])
    ym = jnp.where(is_prompt, ymp_ref[...], yms_ref[...])
    merged = (ga_ref[...] * _dot(ya_ref[...], woa_ref[...])
              + gb_ref[...] * _dot(yb, wob_ref[...])
              + gm_ref[...] * _dot(ym, wom_ref[...]))
    x = x_ref[...] + _dot(merged.astype(BF16), wout_ref[...])
    xo_ref[...] = x
    xnt_ref[...] = _rms(x, g_ref[...]).T.astype(BF16)


def _merge(x, ya, yb_p, yb_s, ym_p, ym_s, ga, gb, gm, woa, wob, wom, wout, g, layer):
    n, d = x.shape
    n_p, n_s = yb_p.shape[0], yb_s.shape[0]
    tm = _tile(math.gcd(n_p, n_s), 256)
    npt, nst = n_p // tm, n_s // tm
    tok = pl.BlockSpec((tm, d), lambda i: (i, 0))
    tok_p = pl.BlockSpec((tm, d), lambda i: (jnp.minimum(i, npt - 1), 0))
    tok_s = pl.BlockSpec((tm, d), lambda i: (jnp.clip(i - npt, 0, nst - 1), 0))
    wgt = pl.BlockSpec((None, d, d), lambda i: (layer, 0, 0))
    return pl.pallas_call(
        functools.partial(_merge_kernel, n_prompt_tiles=npt),
        grid=(n // tm,),
        in_specs=[tok, tok, tok_p, tok_s, tok_p, tok_s, tok, tok, tok] + [wgt] * 4
                 + [pl.BlockSpec((None, 1, d), lambda i: (layer, 0, 0))],
        out_specs=[tok, pl.BlockSpec((d, tm), lambda i: (0, i))],
        out_shape=[jax.ShapeDtypeStruct((n, d), F32), jax.ShapeDtypeStruct((d, n), BF16)],
        compiler_params=_params("arbitrary"),
        name="merge",
    )(x, ya, yb_p, yb_s, ym_p, ym_s, ga, gb, gm, woa, wob, wom, wout, g)


def _extract_topk(s, vals_sc, exact):
    cur = s
    if not exact:
        for r in range(PEER_TOPK):
            m = jnp.max(cur, axis=0, keepdims=True)
            cur = jnp.where(cur == m, -(_MARK_BASE + r * _MARK_STEP), cur)
            vals_sc[r:r + 1, :] = m
        return jnp.where(cur <= -_MARK_BASE, (-cur - _MARK_BASE) * (1.0 / _MARK_STEP), float(PEER_TOPK))
    idx = lax.broadcasted_iota(jnp.int32, s.shape, 0).astype(F32)
    rank = jnp.full(s.shape, float(PEER_TOPK), F32)
    for r in range(PEER_TOPK):
        m = jnp.max(cur, axis=0, keepdims=True)
        first = jnp.min(jnp.where(cur == m, idx, float(s.shape[0])), axis=0, keepdims=True)
        sel = idx == first
        rank = jnp.where(sel, float(r), rank)
        cur = jnp.where(sel, NEG_INF, cur)
        vals_sc[r:r + 1, :] = m
    return rank


def _take_candidates(cand, cand_idx, exact):
    cur = cand
    for _ in range(PEER_TOPK):
        m = jnp.max(cur, axis=0, keepdims=True)
        if exact:
            first = jnp.min(jnp.where(cur == m, cand_idx, float(_CAND_ROWS)), axis=0, keepdims=True)
            cur = jnp.where(cand_idx == first, NEG_INF, cur)
        else:
            cur = jnp.where(cur == m, NEG_INF, cur)
    return jnp.where((cur == NEG_INF) & (cand_idx < float(len(_CAND))), 1.0, 0.0)


def _peer_route_kernel(xnt_ref, wqt_ref, keys_ref, r2_ref, a2_ref, bi_ref, c1_ref,
                       v1_sc, v2_sc, cand_sc, rank_sc, taken_sc, count_sc):
    qt = _dot(wqt_ref[...], xnt_ref[...])
    tl = qt.shape[1]
    cand_idx = lax.broadcasted_iota(jnp.int32, (_CAND_ROWS, tl), 0).astype(F32)
    cand_sc[...] = jnp.full((_CAND_ROWS, tl), NEG_INF, F32)

    def miscount(flags):
        return jnp.abs(jnp.sum(flags, axis=0, keepdims=True) - float(PEER_TOPK))

    for h in range(PEER_HEADS):
        s1 = _dot(keys_ref[2 * h], qt[(2 * h) * PEER_HALF:(2 * h + 1) * PEER_HALF, :].astype(BF16))
        s2 = _dot(keys_ref[2 * h + 1], qt[(2 * h + 1) * PEER_HALF:(2 * h + 2) * PEER_HALF, :].astype(BF16))

        def route(exact):
            rank_sc[0] = _extract_topk(s1, v1_sc, exact)
            rank_sc[1] = _extract_topk(s2, v2_sc, exact)
            for p, (a, b) in enumerate(_CAND):
                cand_sc[p:p + 1, :] = v1_sc[a:a + 1, :] + v2_sc[b:b + 1, :]
            taken = _take_candidates(cand_sc[...], cand_idx, exact)
            taken_sc[...] = taken
            if exact:
                row0 = 0
                for a, n_a in enumerate(_CAND_WIDTH):
                    in_a = (cand_idx >= float(row0)) & (cand_idx < float(row0 + n_a))
                    count_sc[a:a + 1, :] = jnp.sum(jnp.where(in_a, taken, 0.0), axis=0, keepdims=True)
                    row0 += n_a
            else:
                tau = jnp.min(jnp.where(taken > 0.0, cand_sc[...], float("inf")), axis=0, keepdims=True)
                v2 = v2_sc[...]
                b_idx = lax.broadcasted_iota(jnp.int32, v2.shape, 0)
                for a, n_a in enumerate(_CAND_WIDTH):
                    hit = (v1_sc[a:a + 1, :] + v2 >= tau) & (b_idx < n_a)
                    count_sc[a:a + 1, :] = jnp.sum(jnp.where(hit, 1.0, 0.0), axis=0, keepdims=True)

        route(False)
        ranked = lambda r: jnp.where(r < float(PEER_TOPK), 1.0, 0.0)
        tied = jnp.max(miscount(ranked(rank_sc[0])) + miscount(ranked(rank_sc[1])) + miscount(taken_sc[...]))

        @pl.when(tied > 0.0)
        def _():
            route(True)

        rank1, rank2, taken, cand = rank_sc[0], rank_sc[1], taken_sc[...], cand_sc[...]
        top = v1_sc[0:1, :] + v2_sc[0:1, :]
        z = jnp.sum(taken * jnp.exp(cand - top), axis=0, keepdims=True)
        bi = jnp.zeros_like(s1)
        for a in range(PEER_TOPK):
            bi = jnp.where(rank1 == float(a), count_sc[a:a + 1, :], bi)
        r2_ref[h] = rank2.astype(BF16)
        a2_ref[h] = jnp.exp(s2 - v2_sc[0:1, :]).astype(BF16)
        bi_ref[h] = bi
        c1_ref[h] = jnp.exp(s1 - v1_sc[0:1, :]) * (0.5 / z)


def _peer_route(xnt, wqt, keys, layer):
    d, n = xnt.shape
    tl = _tile(n, 256)
    rows = pl.BlockSpec((PEER_HEADS, N_KEYS, tl), lambda i: (0, 0, i))
    return pl.pallas_call(
        _peer_route_kernel,
        grid=(n // tl,),
        in_specs=[pl.BlockSpec((d, tl), lambda i: (0, i)),
                  pl.BlockSpec((None, d, d), lambda i: (layer, 0, 0)),
                  pl.BlockSpec((None, 2 * PEER_HEADS, N_KEYS, PEER_HALF), lambda i: (layer, 0, 0, 0))],
        out_specs=[rows] * 4,
        out_shape=[jax.ShapeDtypeStruct((PEER_HEADS, N_KEYS, n), dt) for dt in (BF16, BF16, F32, F32)],
        scratch_shapes=[pltpu.VMEM((PEER_TOPK, tl), F32), pltpu.VMEM((PEER_TOPK, tl), F32),
                        pltpu.VMEM((_CAND_ROWS, tl), F32), pltpu.VMEM((2, N_KEYS, tl), F32),
                        pltpu.VMEM((_CAND_ROWS, tl), F32), pltpu.VMEM((PEER_TOPK, tl), F32)],
        compiler_params=_params("arbitrary"),
        name="peer_route",
    )(xnt, wqt, keys)


def _peer_dense_kernel(xnt_ref, u_ref, vt_ref, r2_ref, a2_ref, bi_ref, c1_ref, x_ref, g_ref, xo_ref, acc_sc, p_sc,
                       *, final_norm, n_chunks):
    e = pl.program_id(1)
    n_eb = pl.num_programs(1) - 1
    eb, tm = u_ref.shape[0], xnt_ref.shape[1]
    ce = eb // n_chunks
    cur = lax.rem(e, 2)
    prev = 1 - cur
    zero = jnp.zeros((), BF16)

    def row_tile(ref, h, e1):
        row = jnp.broadcast_to(ref[h, pl.ds(e1, 1), :], (BF16_ROWS, tm)).astype(BF16)
        return jnp.concatenate([row] * (N_KEYS // BF16_ROWS), axis=0)

    def build(drain):
        xnt = xnt_ref[...]
        hids = [_dot(u_ref[c * ce:(c + 1) * ce, :], xnt) for c in range(n_chunks)]
        if drain:
            acc_sc[...] += _dot(vt_ref[...], p_sc[prev])
        for c in range(n_chunks):
            act = (hids[c] * (1.0 + lax.erf(hids[c] * (2.0 ** -0.5)))).astype(BF16)
            for sub in range(ce // N_KEYS):
                r0 = c * ce + sub * N_KEYS
                e1 = e * (eb // N_KEYS) + r0 // N_KEYS
                w = None
                for h in range(PEER_HEADS):
                    term = jnp.where(r2_ref[h] < row_tile(bi_ref, h, e1), a2_ref[h], zero) * row_tile(c1_ref, h, e1)
                    w = term if w is None else w + term
                p_sc[cur, r0:r0 + N_KEYS, :] = w * act[sub * N_KEYS:(sub + 1) * N_KEYS, :]

    @pl.when(e == 0)
    def _():
        acc_sc[...] = jnp.zeros_like(acc_sc)
        build(drain=False)

    @pl.when((e > 0) & (e < n_eb))
    def _():
        build(drain=True)

    @pl.when(e == n_eb)
    def _():
        x = x_ref[...] + (acc_sc[...] + _dot(vt_ref[...], p_sc[prev])).T
        xo_ref[...] = _rms(x, g_ref[...]) if final_norm else x


def _peer_dense(xnt, u, vt, r2, a2, bi, c1, x, g, layer, final_norm):
    d, n = xnt.shape
    n_exp = u.shape[1]
    tm = _tile(n, 512, 2 * LANES)
    eb, n_chunks = 2048, 4
    n_eb = n_exp // eb
    assert n_exp % eb == 0 and n_exp == N_KEYS * N_KEYS
    tab = pl.BlockSpec((PEER_HEADS, N_KEYS, tm), lambda i, e: (0, 0, i))
    tok = pl.BlockSpec((tm, d), lambda i, e: (i, 0))
    return pl.pallas_call(
        functools.partial(_peer_dense_kernel, final_norm=final_norm, n_chunks=n_chunks),
        grid=(n // tm, n_eb + 1),
        in_specs=[pl.BlockSpec((d, tm), lambda i, e: (0, i)),
                  pl.BlockSpec((None, eb, d), lambda i, e: (layer, jnp.minimum(e, n_eb - 1), 0)),
                  pl.BlockSpec((None, d, eb), lambda i, e: (layer, 0, jnp.maximum(e - 1, 0))),
                  tab, tab, tab, tab, tok,
                  pl.BlockSpec((1, d), lambda i, e: (0, 0))],
        out_specs=tok,
        out_shape=jax.ShapeDtypeStruct((n, d), F32),
        scratch_shapes=[pltpu.VMEM((d, tm), F32), pltpu.VMEM((2, eb, tm), BF16)],
        compiler_params=_params("arbitrary", "arbitrary"),
        name="peer_dense",
    )(xnt, u, vt, r2, a2, bi, c1, x, g)


def kernel(x_prompt, x_sample, mem_prompt, cache_mem_k, cache_mem_v, state_conv, norm_mix_g, w_in, a_ln_g, a_ln_b, a_ws, a_bs, b_conv_w, b_conv_b, b_ln_g, b_ln_b, mem_norm_g, w_mem_k, w_mem_v, w_o_a, w_o_b, w_o_m, w_out, norm_ffn_g, peer_wq, peer_keys, peer_u, peer_v, final_norm_g):
    batch, seq, d = x_prompt.shape
    dec_batch, dec_seq, _ = x_sample.shape
    depth = w_in.shape[0]
    n_mem = mem_prompt.shape[1]
    n_p, n_s = batch * seq, dec_batch * dec_seq
    hd = d // M_HEADS

    vec = lambda a: a.reshape(depth, 1, -1)
    w_in_b = w_in.astype(BF16)
    wk_b, wv_b = w_mem_k.astype(BF16), w_mem_v.astype(BF16)
    woa_b, wob_b, wom_b, wout_b = (w.astype(BF16) for w in (w_o_a, w_o_b, w_o_m, w_out))
    wqt_b = jnp.swapaxes(peer_wq, 1, 2).astype(BF16)
    keys_b = peer_keys.reshape(depth, 2 * PEER_HEADS, N_KEYS, PEER_HALF).astype(BF16)
    u_b = peer_u.astype(BF16)
    vt_b = jnp.swapaxes(peer_v, 1, 2).astype(BF16)
    conv_w = jnp.pad(b_conv_w, ((0, 0), (0, CONV_HALO - CONV_W), (0, 0)))
    reps = CHUNK // dec_seq
    ws_eff = jnp.stack([a_ws, jnp.tile(a_ws[:, :, :dec_seq, :dec_seq], (1, 1, reps, reps))], axis=1)
    bs_rows = jnp.repeat(jnp.swapaxes(a_bs, 1, 2), d // A_GROUPS, axis=2)
    bs_eff = jnp.stack([bs_rows, jnp.tile(bs_rows[:, :dec_seq], (1, reps, 1))], axis=1)

    mk_p, mv_p = _memkv(mem_prompt.reshape(batch * n_mem, d), vec(mem_norm_g), wk_b, wv_b)
    def lane_rows(c):
        c = c.reshape(depth, dec_batch, n_mem, M_HEADS, hd // LANES, LANES)
        return jnp.swapaxes(c, 3, 4).reshape(depth, dec_batch, n_mem * (hd // LANES) * M_HEADS, LANES)

    cache_k, cache_v = lane_rows(cache_mem_k), lane_rows(cache_mem_v)

    x = jnp.concatenate([x_prompt.reshape(n_p, d), x_sample.reshape(n_s, d)], axis=0)
    conv_p, conv_s, chunk_v_s = [], [], []
    for l in range(depth):
        ya, vn_s, glu, qm, ga, gb, gm = _inproj(x, vec(norm_mix_g), w_in_b, vec(a_ln_g), vec(a_ln_b),
                                                ws_eff, bs_eff, l, n_p, dec_seq)
        ext_s = jnp.concatenate([state_conv[l], glu[n_p:].reshape(dec_batch, dec_seq, d)], axis=1)
        yb_p = _conv_prompt(glu, conv_w, vec(b_conv_b), vec(b_ln_g), vec(b_ln_b), l, batch, seq)
        yb_s = _conv_sample(jnp.swapaxes(ext_s, 0, 1), conv_w, vec(b_conv_b), vec(b_ln_g), vec(b_ln_b), l)
        yb_s = jnp.swapaxes(yb_s, 0, 1).reshape(n_s, d)
        ym_p = _attn_prompt(qm, mk_p, mv_p, l, batch, seq)
        ym_s = _attn_sample(qm, cache_k, cache_v, l, n_p, dec_batch, dec_seq)
        x, xnt = _merge(x, ya, yb_p, yb_s, ym_p, ym_s, ga, gb, gm, woa_b, wob_b, wom_b, wout_b, vec(norm_ffn_g), l)
        r2, a2, bi, c1 = _peer_route(xnt, wqt_b, keys_b, l)
        x = _peer_dense(xnt, u_b, vt_b, r2, a2, bi, c1, x, final_norm_g.reshape(1, d), l, l == depth - 1)
        conv_p.append(jnp.stack([glu[(b + 1) * seq - (CONV_W - 1):(b + 1) * seq] for b in range(batch)]))
        conv_s.append(ext_s[:, dec_seq:])
        chunk_v_s.append(vn_s.reshape(dec_batch, dec_seq, d))

    y_prompt = x[:n_p].reshape(batch, seq, d)
    y_sample = x[n_p:].reshape(dec_batch, dec_seq, d)
    new_mem_k = mk_p.reshape(depth, batch, n_mem, M_HEADS, hd)
    new_mem_v = mv_p.reshape(depth, batch, n_mem, M_HEADS, hd)
    return (y_prompt, y_sample, new_mem_k, new_mem_v,
            jnp.stack(conv_p), jnp.stack(conv_s), jnp.stack(chunk_v_s))
```

```python
import functools
import math

import jax
import jax.numpy as jnp
from jax import lax
from jax.experimental import pallas as pl
from jax.experimental.pallas import tpu as pltpu

F32 = jnp.float32
BF16 = jnp.bfloat16
EPS = 1e-6
LANES = 128
SUBLANES = 8
BF16_ROWS = 16
VMEM_LIMIT_BYTES = 56 * 1024 * 1024

CHUNK = 128
A_GROUPS = 8
CONV_W = 31
CONV_HALO = 32
M_HEADS = 4
PEER_HEADS = 8
N_KEYS = 128
PEER_HALF = 64
PEER_TOPK = 16
NEG_INF = float("-inf")
_MARK_BASE = 2.0 ** 127
_MARK_STEP = 2.0 ** 122

_CAND = [(a, b) for a in range(PEER_TOPK) for b in range(PEER_TOPK) if (a + 1) * (b + 1) <= PEER_TOPK]
_CAND_ROWS = -(-len(_CAND) // 8) * 8
_CAND_WIDTH = [sum(1 for (a, _) in _CAND if a == r) for r in range(PEER_TOPK)]


def _tile(n, target, quantum=LANES):
    best = None
    t = quantum
    while t <= min(n, target):
        if n % t == 0:
            best = t
        t += quantum
    assert best is not None, (n, target, quantum)
    return best


def _params(*sem):
    return pltpu.CompilerParams(dimension_semantics=sem, vmem_limit_bytes=VMEM_LIMIT_BYTES)


def _gelu(x):
    return 0.5 * x * (1.0 + lax.erf(x * (2.0 ** -0.5)))


def _sigmoid(x):
    return 1.0 / (1.0 + jnp.exp(-x))


def _rms(x, g):
    ms = jnp.mean(x * x, axis=-1, keepdims=True)
    return x * lax.rsqrt(ms + EPS) * g


def _layernorm(x, g, b):
    mu = jnp.mean(x, axis=-1, keepdims=True)
    xc = x - mu
    var = jnp.mean(xc * xc, axis=-1, keepdims=True)
    return xc * lax.rsqrt(var + EPS) * g + b


def _dot(a, b):
    return jnp.dot(a, b, preferred_element_type=F32)


def _dot_nt(a, b):
    return lax.dot_general(a, b, (((1,), (1,)), ((), ())), preferred_element_type=F32)


def _memkv_kernel(mem_ref, g_ref, wk_ref, wv_ref, k_ref, v_ref):
    mn = _rms(mem_ref[...], g_ref[...]).astype(BF16)
    k_ref[...] = _dot(mn, wk_ref[...])
    v_ref[...] = _dot(mn, wv_ref[...])


def _memkv(mem, g, wk, wv):
    r, d = mem.shape
    depth, _, dm = wk.shape
    tm = _tile(r, 512)
    return pl.pallas_call(
        _memkv_kernel,
        grid=(depth, r // tm),
        in_specs=[
            pl.BlockSpec((tm, d), lambda l, i: (i, 0)),
            pl.BlockSpec((None, 1, d), lambda l, i: (l, 0, 0)),
            pl.BlockSpec((None, d, dm), lambda l, i: (l, 0, 0)),
            pl.BlockSpec((None, d, dm), lambda l, i: (l, 0, 0)),
        ],
        out_specs=[
            pl.BlockSpec((None, tm, dm), lambda l, i: (l, i, 0)),
            pl.BlockSpec((None, tm, dm), lambda l, i: (l, i, 0)),
        ],
        out_shape=[jax.ShapeDtypeStruct((depth, r, dm), F32)] * 2,
        compiler_params=_params("arbitrary", "arbitrary"),
        name="memkv",
    )(mem, g, wk, wv)


def _inproj_kernel(x_ref, g_ref, w_ref, lng_ref, lnb_ref, ws_ref, bs_ref,
                   ya_ref, vns_ref, glu_ref, qm_ref, ga_ref, gb_ref, gm_ref, xn_sc, ua_sc, val_sc,
                   *, n_prompt_tiles, sample_shift):
    i, j = pl.program_id(0), pl.program_id(1)
    n_chunks = 4
    cw = w_ref.shape[1] // n_chunks

    @pl.when(j == 0)
    def _():
        xn_sc[...] = _rms(x_ref[...], g_ref[...]).astype(BF16)

    def project():
        xn = xn_sc[...]
        return [_dot(xn, w_ref[:, c * cw:(c + 1) * cw]) for c in range(n_chunks)]

    def segment(seg, out_ref, fn):
        @pl.when(j == seg)
        def _():
            for c, h in enumerate(project()):
                cs = slice(c * cw, (c + 1) * cw)
                out_ref[:, cs] = fn(h, cs).astype(out_ref.dtype)

    segment(0, ua_sc, lambda h, cs: _gelu(h))

    @pl.when(j == 1)
    def _():
        v = jnp.concatenate([_gelu(h) for h in project()], axis=1)
        vn = _layernorm(v, lng_ref[...], lnb_ref[...])
        vns_ref[...] = vn
        row = lax.broadcasted_iota(jnp.int32, (CHUNK, CHUNK), 0)
        col = lax.broadcasted_iota(jnp.int32, (CHUNK, CHUNK), 1)
        same_seq = lax.shift_right_logical(row, sample_shift) == lax.shift_right_logical(col, sample_shift)
        keep = (col <= row) & (same_seq | (i < n_prompt_tiles))
        vb = vn.astype(BF16)
        for grp in range(A_GROUPS):
            sl = slice(grp * LANES, (grp + 1) * LANES)
            w = jnp.where(keep, ws_ref[grp], 0.0).astype(BF16)
            for r0 in range(0, vn.shape[0], CHUNK):
                s = _dot(w, vb[r0:r0 + CHUNK, sl]) + bs_ref[:, sl]
                ya_ref[r0:r0 + CHUNK, sl] = (ua_sc[r0:r0 + CHUNK, sl].astype(F32) * s).astype(BF16)

    segment(2, val_sc, lambda h, cs: h)
    segment(3, glu_ref, lambda h, cs: val_sc[:, cs] * _sigmoid(h))
    segment(4, qm_ref, lambda h, cs: h)
    segment(5, ga_ref, lambda h, cs: _sigmoid(h))
    segment(6, gb_ref, lambda h, cs: _sigmoid(h))
    segment(7, gm_ref, lambda h, cs: _sigmoid(h))


def _inproj(x, g, w_in, lng, lnb, ws_eff, bs_eff, layer, n_prompt, dec_seq):
    n, d = x.shape
    n_s = n - n_prompt
    tm = _tile(math.gcd(n_prompt, n_s), 512)
    npt = n_prompt // tm
    nseg = w_in.shape[2] // d
    shift = int(math.log2(dec_seq))
    assert nseg == 8 and 1 << shift == dec_seq and CHUNK % dec_seq == 0
    tok = pl.BlockSpec((tm, d), lambda i, j: (i, 0))
    vec = pl.BlockSpec((None, 1, d), lambda i, j: (layer, 0, 0))
    sel = lambda i: jnp.where(i >= npt, 1, 0)
    return pl.pallas_call(
        functools.partial(_inproj_kernel, n_prompt_tiles=npt, sample_shift=shift),
        grid=(n // tm, nseg),
        in_specs=[tok, vec, pl.BlockSpec((None, d, d), lambda i, j: (layer, 0, j)), vec, vec,
                  pl.BlockSpec((None, None, A_GROUPS, CHUNK, CHUNK), lambda i, j: (layer, sel(i), 0, 0, 0)),
                  pl.BlockSpec((None, None, CHUNK, d), lambda i, j: (layer, sel(i), 0, 0))],
        out_specs=[tok, pl.BlockSpec((tm, d), lambda i, j: (jnp.maximum(i - npt, 0), 0))] + [tok] * 5,
        out_shape=[jax.ShapeDtypeStruct((n, d), BF16), jax.ShapeDtypeStruct((n_s, d), F32)]
                  + [jax.ShapeDtypeStruct((n, d), dt) for dt in (F32, BF16, BF16, BF16, BF16)],
        scratch_shapes=[pltpu.VMEM((tm, d), BF16), pltpu.VMEM((tm, d), BF16), pltpu.VMEM((tm, d), F32)],
        compiler_params=_params("arbitrary", "arbitrary"),
        name="inproj",
    )(x, g, w_in, lng, lnb, ws_eff, bs_eff)


def _conv_prompt_kernel(cur_ref, halo_ref, w_ref, cb_ref, g_ref, b_ref, yb_ref, ext_sc, c_sc):
    t = pl.program_id(1)
    tc, d = cur_ref.shape
    halo = halo_ref[...]
    ext_sc[0:CONV_HALO, :] = jnp.where(t == 0, 0.0, halo)
    ext_sc[CONV_HALO:, :] = cur_ref[...]
    off = CONV_HALO - (CONV_W - 1)
    for cb in range(d // LANES):
        sl = slice(cb * LANES, (cb + 1) * LANES)
        acc = jnp.zeros((tc, LANES), F32)
        for k in range(CONV_W):
            acc = acc + ext_sc[off + k:off + k + tc, sl] * w_ref[k:k + 1, sl]
        c_sc[:, sl] = acc
    y = _layernorm(c_sc[...] + cb_ref[...], g_ref[...], b_ref[...])
    yb_ref[...] = (y * _sigmoid(y)).astype(BF16)


def _conv_prompt(glu, w, cb, g, b, layer, batch, seq):
    d = glu.shape[1]
    tc = _tile(seq, 256)
    nt = seq // tc
    hpt = tc // CONV_HALO
    vec = pl.BlockSpec((None, 1, d), lambda n, t: (layer, 0, 0))
    return pl.pallas_call(
        _conv_prompt_kernel,
        grid=(batch, nt),
        in_specs=[
            pl.BlockSpec((tc, d), lambda n, t: (n * nt + t, 0)),
            pl.BlockSpec((CONV_HALO, d), lambda n, t: (jnp.maximum((n * nt + t) * hpt - 1, 0), 0)),
            pl.BlockSpec((None, CONV_HALO, d), lambda n, t: (layer, 0, 0)),
            vec, vec, vec,
        ],
        out_specs=pl.BlockSpec((tc, d), lambda n, t: (n * nt + t, 0)),
        out_shape=jax.ShapeDtypeStruct((batch * seq, d), BF16),
        scratch_shapes=[pltpu.VMEM((CONV_HALO + tc, d), F32), pltpu.VMEM((tc, d), F32)],
        compiler_params=_params("arbitrary", "arbitrary"),
        name="conv_prompt",
    )(glu, glu, w, cb, g, b)


def _conv_sample_kernel(ext_ref, w_ref, cb_ref, g_ref, b_ref, yb_ref):
    t_new = yb_ref.shape[0]
    for t in range(t_new):
        acc = jnp.zeros(ext_ref.shape[1:], F32)
        for k in range(CONV_W):
            acc = acc + ext_ref[t + k] * w_ref[k:k + 1, :]
        y = _layernorm(acc + cb_ref[...], g_ref[...], b_ref[...])
        yb_ref[t] = (y * _sigmoid(y)).astype(BF16)


def _conv_sample(ext_t, w, cb, g, b, layer):
    te, s, d = ext_t.shape
    t_new = te - (CONV_W - 1)
    sb = 32 if s % 32 == 0 else s
    vec = pl.BlockSpec((None, 1, d), lambda i: (layer, 0, 0))
    return pl.pallas_call(
        _conv_sample_kernel,
        grid=(s // sb,),
        in_specs=[pl.BlockSpec((te, sb, d), lambda i: (0, i, 0)),
                  pl.BlockSpec((None, CONV_HALO, d), lambda i: (layer, 0, 0)),
                  vec, vec, vec],
        out_specs=pl.BlockSpec((t_new, sb, d), lambda i: (0, i, 0)),
        out_shape=jax.ShapeDtypeStruct((t_new, s, d), BF16),
        compiler_params=_params("arbitrary"),
        name="conv_sample",
    )(ext_t, w, cb, g, b)


def _attend(q, k, v, hd):
    outs = []
    for h in range(q.shape[1] // hd):
        sl = slice(h * hd, (h + 1) * hd)
        s = _dot_nt(q[:, sl], k[:, sl].astype(BF16)) * (hd ** -0.5)
        s = s - jnp.max(s, axis=-1, keepdims=True)
        p = jnp.exp(s)
        p = p / jnp.sum(p, axis=-1, keepdims=True)
        outs.append(_dot(p.astype(BF16), v[:, sl].astype(BF16)))
    return outs


def _attn_prompt_kernel(q_ref, k_ref, v_ref, ym_ref, *, hd):
    outs = _attend(q_ref[...], k_ref[...], v_ref[...], hd)
    for h, o in enumerate(outs):
        ym_ref[:, h * hd:(h + 1) * hd] = o.astype(BF16)


def _attn_prompt(qm, k, v, layer, batch, seq):
    dm = qm.shape[1]
    n_mem = k.shape[1] // batch
    tq = _tile(seq, 512)
    nt = seq // tq
    kv = pl.BlockSpec((None, n_mem, dm), lambda n, t: (layer, n, 0))
    return pl.pallas_call(
        functools.partial(_attn_prompt_kernel, hd=dm // M_HEADS),
        grid=(batch, nt),
        in_specs=[pl.BlockSpec((tq, dm), lambda n, t: (n * nt + t, 0)), kv, kv],
        out_specs=pl.BlockSpec((tq, dm), lambda n, t: (n * nt + t, 0)),
        out_shape=jax.ShapeDtypeStruct((batch * seq, dm), BF16),
        compiler_params=_params("arbitrary", "arbitrary"),
        name="attn_prompt",
    )(qm, k, v)


def _attn_sample_kernel(q_ref, k_ref, v_ref, ym_ref, *, hd, dec_seq, n_mem):
    rows = q_ref.shape[0]
    n_half = hd // LANES
    stride = n_half * M_HEADS
    q = q_ref[...]
    seq_of_row = lax.shift_right_logical(lax.broadcasted_iota(jnp.int32, (rows, LANES), 0), int(math.log2(dec_seq)))
    for h in range(M_HEADS):
        acc = [jnp.zeros((rows, LANES), F32) for _ in range(n_half)]
        for s in range(rows // dec_seq):
            sc = None
            for c in range(n_half):
                kc = k_ref[s, pl.ds(c * M_HEADS + h, n_mem, stride=stride), :].astype(BF16)
                part = _dot_nt(q[:, h * hd + c * LANES:h * hd + (c + 1) * LANES], kc)
                sc = part if sc is None else sc + part
            sc = sc * (hd ** -0.5)
            sc = sc - jnp.max(sc, axis=-1, keepdims=True)
            p = jnp.exp(sc)
            p = (p / jnp.sum(p, axis=-1, keepdims=True)).astype(BF16)
            for c in range(n_half):
                vc = v_ref[s, pl.ds(c * M_HEADS + h, n_mem, stride=stride), :].astype(BF16)
                acc[c] = jnp.where(seq_of_row == s, _dot(p, vc), acc[c])
        for c in range(n_half):
            ym_ref[:, h * hd + c * LANES:h * hd + (c + 1) * LANES] = acc[c].astype(BF16)


def _attn_sample(qm, k, v, layer, row0, dec_batch, dec_seq):
    dm = qm.shape[1]
    n_mem = k.shape[2] * LANES // dm
    rows = 16
    spb = rows // dec_seq
    assert rows % dec_seq == 0 and dec_batch % spb == 0 and row0 % rows == 0
    kv = pl.BlockSpec((None, spb, k.shape[2], LANES), lambda i: (layer, i, 0, 0))
    return pl.pallas_call(
        functools.partial(_attn_sample_kernel, hd=dm // M_HEADS, dec_seq=dec_seq, n_mem=n_mem),
        grid=(dec_batch // spb,),
        in_specs=[pl.BlockSpec((rows, dm), lambda i: (row0 // rows + i, 0)), kv, kv],
        out_specs=pl.BlockSpec((rows, dm), lambda i: (i, 0)),
        out_shape=jax.ShapeDtypeStruct((dec_batch * dec_seq, dm), BF16),
        compiler_params=_params("arbitrary"),
        name="attn_sample",
    )(qm, k, v)


def _merge_kernel(x_ref, ya_ref, ybp_ref, ybs_ref, ymp_ref, yms_ref, ga_ref, gb_ref, gm_ref,
                  woa_ref, wob_ref, wom_ref, wout_ref, g_ref, xo_ref, xnt_ref, *, n_prompt_tiles):
    is_prompt = pl.program_id(0) < n_prompt_tiles
    yb = jnp.where(is_prompt, ybp_ref[...], ybs_ref[...])
    ym = jnp.where(is_prompt, ymp_ref[...], yms_ref[...])
    merged = (ga_ref[...] * _dot(ya_ref[...], woa_ref[...])
              + gb_ref[...] * _dot(yb, wob_ref[...])
              + gm_ref[...] * _dot(ym, wom_ref[...]))
    x = x_ref[...] + _dot(merged.astype(BF16), wout_ref[...])
    xo_ref[...] = x
    xnt_ref[...] = _rms(x, g_ref[...]).T.astype(BF16)


def _merge(x, ya, yb_p, yb_s, ym_p, ym_s, ga, gb, gm, woa, wob, wom, wout, g, layer):
    n, d = x.shape
    n_p, n_s = yb_p.shape[0], yb_s.shape[0]
    tm = _tile(math.gcd(n_p, n_s), 512)
    npt, nst = n_p // tm, n_s // tm
    tok = pl.BlockSpec((tm, d), lambda i: (i, 0))
    tok_p = pl.BlockSpec((tm, d), lambda i: (jnp.minimum(i, npt - 1), 0))
    tok_s = pl.BlockSpec((tm, d), lambda i: (jnp.clip(i - npt, 0, nst - 1), 0))
    wgt = pl.BlockSpec((None, d, d), lambda i: (layer, 0, 0))
    return pl.pallas_call(
        functools.partial(_merge_kernel, n_prompt_tiles=npt),
        grid=(n // tm,),
        in_specs=[tok, tok, tok_p, tok_s, tok_p, tok_s, tok, tok, tok] + [wgt] * 4
                 + [pl.BlockSpec((None, 1, d), lambda i: (layer, 0, 0))],
        out_specs=[tok, pl.BlockSpec((d, tm), lambda i: (0, i))],
        out_shape=[jax.ShapeDtypeStruct((n, d), F32), jax.ShapeDtypeStruct((d, n), BF16)],
        compiler_params=_params("arbitrary"),
        name="merge",
    )(x, ya, yb_p, yb_s, ym_p, ym_s, ga, gb, gm, woa, wob, wom, wout, g)


def _extract_topk(s, vals_sc, exact):
    cur = s
    if not exact:
        for r in range(PEER_TOPK):
            m = jnp.max(cur, axis=0, keepdims=True)
            cur = jnp.where(cur == m, -(_MARK_BASE + r * _MARK_STEP), cur)
            vals_sc[r:r + 1, :] = m
        return jnp.where(cur <= -_MARK_BASE, (-cur - _MARK_BASE) * (1.0 / _MARK_STEP), float(PEER_TOPK))
    idx = lax.broadcasted_iota(jnp.int32, s.shape, 0).astype(F32)
    rank = jnp.full(s.shape, float(PEER_TOPK), F32)
    for r in range(PEER_TOPK):
        m = jnp.max(cur, axis=0, keepdims=True)
        first = jnp.min(jnp.where(cur == m, idx, float(s.shape[0])), axis=0, keepdims=True)
        sel = idx == first
        rank = jnp.where(sel, float(r), rank)
        cur = jnp.where(sel, NEG_INF, cur)
        vals_sc[r:r + 1, :] = m
    return rank


def _take_candidates(cand, cand_idx, exact):
    cur = cand
    for _ in range(PEER_TOPK):
        m = jnp.max(cur, axis=0, keepdims=True)
        if exact:
            first = jnp.min(jnp.where(cur == m, cand_idx, float(_CAND_ROWS)), axis=0, keepdims=True)
            cur = jnp.where(cand_idx == first, NEG_INF, cur)
        else:
            cur = jnp.where(cur == m, NEG_INF, cur)
    return jnp.where((cur == NEG_INF) & (cand_idx < float(len(_CAND))), 1.0, 0.0)


def _peer_route_kernel(xnt_ref, wqt_ref, keys_ref, r2_ref, a2_ref, bi_ref, c1_ref,
                       v1_sc, v2_sc, cand_sc, rank_sc, taken_sc, count_sc):
    qt = _dot(wqt_ref[...], xnt_ref[...])
    tl = qt.shape[1]
    cand_idx = lax.broadcasted_iota(jnp.int32, (_CAND_ROWS, tl), 0).astype(F32)
    cand_sc[...] = jnp.full((_CAND_ROWS, tl), NEG_INF, F32)

    def miscount(flags):
        return jnp.abs(jnp.sum(flags, axis=0, keepdims=True) - float(PEER_TOPK))

    for h in range(PEER_HEADS):
        s1 = _dot(keys_ref[2 * h], qt[(2 * h) * PEER_HALF:(2 * h + 1) * PEER_HALF, :].astype(BF16))
        s2 = _dot(keys_ref[2 * h + 1], qt[(2 * h + 1) * PEER_HALF:(2 * h + 2) * PEER_HALF, :].astype(BF16))

        def route(exact):
            rank_sc[0] = _extract_topk(s1, v1_sc, exact)
            rank_sc[1] = _extract_topk(s2, v2_sc, exact)
            for p, (a, b) in enumerate(_CAND):
                cand_sc[p:p + 1, :] = v1_sc[a:a + 1, :] + v2_sc[b:b + 1, :]
            taken = _take_candidates(cand_sc[...], cand_idx, exact)
            taken_sc[...] = taken
            if exact:
                row0 = 0
                for a, n_a in enumerate(_CAND_WIDTH):
                    in_a = (cand_idx >= float(row0)) & (cand_idx < float(row0 + n_a))
                    count_sc[a:a + 1, :] = jnp.sum(jnp.where(in_a, taken, 0.0), axis=0, keepdims=True)
                    row0 += n_a
            else:
                tau = jnp.min(jnp.where(taken > 0.0, cand_sc[...], float("inf")), axis=0, keepdims=True)
                v2 = v2_sc[...]
                b_idx = lax.broadcasted_iota(jnp.int32, v2.shape, 0)
                for a, n_a in enumerate(_CAND_WIDTH):
                    hit = (v1_sc[a:a + 1, :] + v2 >= tau) & (b_idx < n_a)
                    count_sc[a:a + 1, :] = jnp.sum(jnp.where(hit, 1.0, 0.0), axis=0, keepdims=True)

        route(False)
        ranked = lambda r: jnp.where(r < float(PEER_TOPK), 1.0, 0.0)
        tied = jnp.max(miscount(ranked(rank_sc[0])) + miscount(ranked(rank_sc[1])) + miscount(taken_sc[...]))

        @pl.when(tied > 0.0)
        def _():
            route(True)

        rank1, rank2, taken, cand = rank_sc[0], rank_sc[1], taken_sc[...], cand_sc[...]
        top = v1_sc[0:1, :] + v2_sc[0:1, :]
        z = jnp.sum(taken * jnp.exp(cand - top), axis=0, keepdims=True)
        bi = jnp.zeros_like(s1)
        for a in range(PEER_TOPK):
            bi = jnp.where(rank1 == float(a), count_sc[a:a + 1, :], bi)
        r2_ref[h] = rank2.astype(BF16)
        a2_ref[h] = jnp.exp(s2 - v2_sc[0:1, :]).astype(BF16)
        bi_ref[h] = bi
        c1_ref[h] = jnp.exp(s1 - v1_sc[0:1, :]) * (0.5 / z)


def _peer_route(xnt, wqt, keys, layer):
    d, n = xnt.shape
    tl = _tile(n, 256)
    rows = pl.BlockSpec((PEER_HEADS, N_KEYS, tl), lambda i: (0, 0, i))
    return pl.pallas_call(
        _peer_route_kernel,
        grid=(n // tl,),
        in_specs=[pl.BlockSpec((d, tl), lambda i: (0, i)),
                  pl.BlockSpec((None, d, d), lambda i: (layer, 0, 0)),
                  pl.BlockSpec((None, 2 * PEER_HEADS, N_KEYS, PEER_HALF), lambda i: (layer, 0, 0, 0))],
        out_specs=[rows] * 4,
        out_shape=[jax.ShapeDtypeStruct((PEER_HEADS, N_KEYS, n), dt) for dt in (BF16, BF16, F32, F32)],
        scratch_shapes=[pltpu.VMEM((PEER_TOPK, tl), F32), pltpu.VMEM((PEER_TOPK, tl), F32),
                        pltpu.VMEM((_CAND_ROWS, tl), F32), pltpu.VMEM((2, N_KEYS, tl), F32),
                        pltpu.VMEM((_CAND_ROWS, tl), F32), pltpu.VMEM((PEER_TOPK, tl), F32)],
        compiler_params=_params("arbitrary"),
        name="peer_route",
    )(xnt, wqt, keys)


def _peer_dense_kernel(xnt_ref, u_ref, vt_ref, r2_ref, a2_ref, bi_ref, c1_ref, x_ref, g_ref, xo_ref, acc_sc, p_sc,
                       *, final_norm, n_chunks):
    e = pl.program_id(1)
    n_eb = pl.num_programs(1) - 1
    eb, tm = u_ref.shape[0], xnt_ref.shape[1]
    ce = eb // n_chunks
    cur = lax.rem(e, 2)
    prev = 1 - cur
    zero = jnp.zeros((), BF16)

    def row_tile(ref, h, e1):
        row = jnp.broadcast_to(ref[h, pl.ds(e1, 1), :], (BF16_ROWS, tm)).astype(BF16)
        return jnp.concatenate([row] * (N_KEYS // BF16_ROWS), axis=0)

    def build(drain):
        xnt = xnt_ref[...]
        hids = [_dot(u_ref[c * ce:(c + 1) * ce, :], xnt) for c in range(n_chunks)]
        if drain:
            acc_sc[...] += _dot(vt_ref[...], p_sc[prev])
        for c in range(n_chunks):
            act = (hids[c] * (1.0 + lax.erf(hids[c] * (2.0 ** -0.5)))).astype(BF16)
            for sub in range(ce // N_KEYS):
                r0 = c * ce + sub * N_KEYS
                e1 = e * (eb // N_KEYS) + r0 // N_KEYS
                w = None
                for h in range(PEER_HEADS):
                    term = jnp.where(r2_ref[h] < row_tile(bi_ref, h, e1), a2_ref[h], zero) * row_tile(c1_ref, h, e1)
                    w = term if w is None else w + term
                p_sc[cur, r0:r0 + N_KEYS, :] = w * act[sub * N_KEYS:(sub + 1) * N_KEYS, :]

    @pl.when(e == 0)
    def _():
        acc_sc[...] = jnp.zeros_like(acc_sc)
        build(drain=False)

    @pl.when((e > 0) & (e < n_eb))
    def _():
        build(drain=True)

    @pl.when(e == n_eb)
    def _():
        x = x_ref[...] + (acc_sc[...] + _dot(vt_ref[...], p_sc[prev])).T
        xo_ref[...] = _rms(x, g_ref[...]) if final_norm else x


def _peer_dense(xnt, u, vt, r2, a2, bi, c1, x, g, layer, final_norm):
    d, n = xnt.shape
    n_exp = u.shape[1]
    tm = _tile(n, 512, 2 * LANES)
    eb, n_chunks = 2048, 16
    n_eb = n_exp // eb
    assert n_exp % eb == 0 and n_exp == N_KEYS * N_KEYS
    tab = pl.BlockSpec((PEER_HEADS, N_KEYS, tm), lambda i, e: (0, 0, i))
    tok = pl.BlockSpec((tm, d), lambda i, e: (i, 0))
    return pl.pallas_call(
        functools.partial(_peer_dense_kernel, final_norm=final_norm, n_chunks=n_chunks),
        grid=(n // tm, n_eb + 1),
        in_specs=[pl.BlockSpec((d, tm), lambda i, e: (0, i)),
                  pl.BlockSpec((None, eb, d), lambda i, e: (layer, jnp.minimum(e, n_eb - 1), 0)),
                  pl.BlockSpec((None, d, eb), lambda i, e: (layer, 0, jnp.maximum(e - 1, 0))),
                  tab, tab, tab, tab, tok,
                  pl.BlockSpec((1, d), lambda i, e: (0, 0))],
        out_specs=tok,
        out_shape=jax.ShapeDtypeStruct((n, d), F32),
        scratch_shapes=[pltpu.VMEM((d, tm), F32), pltpu.VMEM((2, eb, tm), BF16)],
        compiler_params=_params("arbitrary", "arbitrary"),
        name="peer_dense",
    )(xnt, u, vt, r2, a2, bi, c1, x, g)


def kernel(x_prompt, x_sample, mem_prompt, cache_mem_k, cache_mem_v, state_conv, norm_mix_g, w_in, a_ln_g, a_ln_b, a_ws, a_bs, b_conv_w, b_conv_b, b_ln_g, b_ln_b, mem_norm_g, w_mem_k, w_mem_v, w_o_a, w_o_b, w_o_m, w_out, norm_ffn_g, peer_wq, peer_keys, peer_u, peer_v, final_norm_g):
    batch, seq, d = x_prompt.shape
    dec_batch, dec_seq, _ = x_sample.shape
    depth = w_in.shape[0]
    n_mem = mem_prompt.shape[1]
    n_p, n_s = batch * seq, dec_batch * dec_seq
    hd = d // M_HEADS

    vec = lambda a: a.reshape(depth, 1, -1)
    w_in_b = w_in.astype(BF16)
    wk_b, wv_b = w_mem_k.astype(BF16), w_mem_v.astype(BF16)
    woa_b, wob_b, wom_b, wout_b = (w.astype(BF16) for w in (w_o_a, w_o_b, w_o_m, w_out))
    wqt_b = jnp.swapaxes(peer_wq, 1, 2).astype(BF16)
    keys_b = peer_keys.reshape(depth, 2 * PEER_HEADS, N_KEYS, PEER_HALF).astype(BF16)
    u_b = peer_u.astype(BF16)
    vt_b = jnp.swapaxes(peer_v, 1, 2).astype(BF16)
    conv_w = jnp.pad(b_conv_w, ((0, 0), (0, CONV_HALO - CONV_W), (0, 0)))
    reps = CHUNK // dec_seq
    ws_eff = jnp.stack([a_ws, jnp.tile(a_ws[:, :, :dec_seq, :dec_seq], (1, 1, reps, reps))], axis=1)
    bs_rows = jnp.repeat(jnp.swapaxes(a_bs, 1, 2), d // A_GROUPS, axis=2)
    bs_eff = jnp.stack([bs_rows, jnp.tile(bs_rows[:, :dec_seq], (1, reps, 1))], axis=1)

    mk_p, mv_p = _memkv(mem_prompt.reshape(batch * n_mem, d), vec(mem_norm_g), wk_b, wv_b)
    def lane_rows(c):
        c = c.reshape(depth, dec_batch, n_mem, M_HEADS, hd // LANES, LANES)
        return jnp.swapaxes(c, 3, 4).reshape(depth, dec_batch, n_mem * (hd // LANES) * M_HEADS, LANES)

    cache_k, cache_v = lane_rows(cache_mem_k), lane_rows(cache_mem_v)

    x = jnp.concatenate([x_prompt.reshape(n_p, d), x_sample.reshape(n_s, d)], axis=0)
    conv_p, conv_s, chunk_v_s = [], [], []
    for l in range(depth):
        ya, vn_s, glu, qm, ga, gb, gm = _inproj(x, vec(norm_mix_g), w_in_b, vec(a_ln_g), vec(a_ln_b),
                                                ws_eff, bs_eff, l, n_p, dec_seq)
        ext_s = jnp.concatenate([state_conv[l], glu[n_p:].reshape(dec_batch, dec_seq, d)], axis=1)
        yb_p = _conv_prompt(glu, conv_w, vec(b_conv_b), vec(b_ln_g), vec(b_ln_b), l, batch, seq)
        yb_s = _conv_sample(jnp.swapaxes(ext_s, 0, 1), conv_w, vec(b_conv_b), vec(b_ln_g), vec(b_ln_b), l)
        yb_s = jnp.swapaxes(yb_s, 0, 1).reshape(n_s, d)
        ym_p = _attn_prompt(qm, mk_p, mv_p, l, batch, seq)
        ym_s = _attn_sample(qm, cache_k, cache_v, l, n_p, dec_batch, dec_seq)
        x, xnt = _merge(x, ya, yb_p, yb_s, ym_p, ym_s, ga, gb, gm, woa_b, wob_b, wom_b, wout_b, vec(norm_ffn_g), l)
        r2, a2, bi, c1 = _peer_route(xnt, wqt_b, keys_b, l)
        x = _peer_dense(xnt, u_b, vt_b, r2, a2, bi, c1, x, final_norm_g.reshape(1, d), l, l == depth - 1)
        conv_p.append(jnp.stack([glu[(b + 1) * seq - (CONV_W - 1):(b + 1) * seq] for b in range(batch)]))
        conv_s.append(ext_s[:, dec_seq:])
        chunk_v_s.append(vn_s.reshape(dec_batch, dec_seq, d))

    y_prompt = x[:n_p].reshape(batch, seq, d)
    y_sample = x[n_p:].reshape(dec_batch, dec_seq, d)
    new_mem_k = mk_p.reshape(depth, batch, n_mem, M_HEADS, hd)
    new_mem_v = mv_p.reshape(depth, batch, n_mem, M_HEADS, hd)
    return (y_prompt, y_sample, new_mem_k, new_mem_v,
            jnp.stack(conv_p), jnp.stack(conv_s), jnp.stack(chunk_v_s))
```
